```python
import math
import jax, jax.numpy as jnp
from jax import lax
import numpy as np

D_MODEL = 4096
BATCH = 2
SEQ = 8192
DEPTH = 4

N_A_LAYERS = DEPTH // 2
N_B_LAYERS = DEPTH - N_A_LAYERS
DIFF_HEADS = 16
DIFF_HEAD_DIM = D_MODEL // DIFF_HEADS // 2
MLA_HEADS = D_MODEL // 128
MLA_Q_RANK = 1024
MLA_KV_RANK = 512
MLA_NOPE_DIM = 128
MLA_ROPE_DIM = 64
MLA_V_DIM = 128
N_EXPERTS = 64
N_GROUPS = 8
TOPK_GROUPS = 4
TOP_K = 8
EXPERT_DIM = 192
SHARED_DIM = 192
ROUTED_SCALE = 2.5
EXPERT_BLOCK = 256
Q_BLOCK = 128
ROPE_THETA = 10000.0
LN_EPS = 1e-5
RMS_EPS = 1e-6
DEEPNORM_ALPHA = (2 * DEPTH) ** 0.25
DEEPNORM_BETA = (8 * DEPTH) ** -0.25

kernel_name = 'yoco_diffattn_mla_moe_deepnorm'

F32 = jnp.float32


def layer_norm(x, g, b):
    xf = x.astype(F32)
    mu = jnp.mean(xf, axis=-1, keepdims=True)
    xc = xf - mu
    var = jnp.mean(xc * xc, axis=-1, keepdims=True)
    return (xc * lax.rsqrt(var + LN_EPS) * g.astype(F32) + b.astype(F32)).astype(x.dtype)


def rms_norm(x, g, eps):
    xf = x.astype(F32)
    ms = jnp.mean(xf * xf, axis=-1, keepdims=True)
    return (xf * lax.rsqrt(ms + eps) * g.astype(F32)).astype(x.dtype)


def rope(x):
    S, d = x.shape[1], x.shape[-1]
    inv = ROPE_THETA ** (-jnp.arange(0, d, 2, dtype=F32) / d)
    ang = jnp.arange(S, dtype=F32)[:, None] * inv[None, :]
    cos = jnp.cos(ang)[None, :, None, :]
    sin = jnp.sin(ang)[None, :, None, :]
    xf = x.astype(F32)
    x1, x2 = xf[..., : d // 2], xf[..., d // 2:]
    return jnp.concatenate([x1 * cos - x2 * sin, x2 * cos + x1 * sin], axis=-1).astype(x.dtype)


def causal_attention(q, k, v, map_w):
    B, S, H, M, dk = q.shape
    nb = S // Q_BLOCK
    qb = q.reshape(B, nb, Q_BLOCK, H, M, dk).transpose(1, 0, 2, 3, 4, 5)
    kpos = jnp.arange(S)
    mw = map_w.astype(F32)

    def one_block(args):
        q_blk, i = args
        s = jnp.einsum('bqhmd,bkhmd->bhmqk', q_blk, k, preferred_element_type=F32)
        qpos = i * Q_BLOCK + jnp.arange(Q_BLOCK)
        s = jnp.where(kpos[None, :] <= qpos[:, None], s, -jnp.inf)
        p = jax.nn.softmax(s, axis=-1)
        p = jnp.einsum('bhmqk,m->bhqk', p, mw)
        return jnp.einsum('bhqk,bkhd->bqhd', p.astype(v.dtype), v)

    out = lax.map(one_block, (qb, jnp.arange(nb)))
    return out.transpose(1, 0, 2, 3, 4).reshape(B, S, H, v.shape[-1])


def diff_attention(x, w_qkv, lam, subln, w_o, lambda_init):
    B, S, D = x.shape
    H, d = DIFF_HEADS, DIFF_HEAD_DIM
    q, k, v = jnp.split(x @ w_qkv, 3, axis=-1)
    q = (rope(q.reshape(B, S, 2 * H, d)) * (d ** -0.5)).reshape(B, S, H, 2, d)
    k = rope(k.reshape(B, S, 2 * H, d)).reshape(B, S, H, 2, d)
    v = v.reshape(B, S, H, 2 * d)
    lf = lam.astype(F32)
    lam_full = jnp.exp(jnp.sum(lf[0] * lf[1])) - jnp.exp(jnp.sum(lf[2] * lf[3])) + lambda_init
    map_w = jnp.stack([jnp.ones((), F32), -lam_full])
    o = causal_attention(q, k, v, map_w)
    o = rms_norm(o, subln, LN_EPS) * (1.0 - lambda_init)
    return o.reshape(B, S, D) @ w_o


def mla_shared_kv(x, w_a, kv_norm, w_b):
    B, S, _ = x.shape
    H = MLA_HEADS
    kv_a = x @ w_a
    c_kv = rms_norm(kv_a[..., :MLA_KV_RANK], kv_norm, RMS_EPS)
    k_pe = rope(kv_a[..., MLA_KV_RANK:][:, :, None, :])
    kv = (c_kv @ w_b).reshape(B, S, H, MLA_NOPE_DIM + MLA_V_DIM)
    k_nope, v = kv[..., :MLA_NOPE_DIM], kv[..., MLA_NOPE_DIM:]
    k = jnp.concatenate([k_nope, jnp.broadcast_to(k_pe, (B, S, H, MLA_ROPE_DIM))], axis=-1)
    return k[:, :, :, None, :], v


def mla_attention(x, k, v, w_dq, q_norm, w_uq, w_o):
    B, S, _ = x.shape
    H = MLA_HEADS
    c_q = rms_norm(x @ w_dq, q_norm, RMS_EPS)
    q = (c_q @ w_uq).reshape(B, S, H, MLA_NOPE_DIM + MLA_ROPE_DIM)
    q = jnp.concatenate([q[..., :MLA_NOPE_DIM], rope(q[..., MLA_NOPE_DIM:])], axis=-1)
    q = q * ((MLA_NOPE_DIM + MLA_ROPE_DIM) ** -0.5)
    o = causal_attention(q[:, :, :, None, :], k, v, jnp.ones((1,), F32))
    return o.reshape(B, S, H * MLA_V_DIM) @ w_o


def swiglu(x, w_gu, w_d):
    g, u = jnp.split(x @ w_gu, 2, axis=-1)
    return (jax.nn.silu(g) * u) @ w_d


def routed_experts(xf, e_idx, gate, w_gate_up, w_down):
    N, D = xf.shape
    A = N * TOP_K
    C = EXPERT_BLOCK
    n_blocks = -(-A // C) + N_EXPERTS
    flat_e = e_idx.reshape(-1).astype(jnp.int32)
    flat_tok = jnp.arange(A, dtype=jnp.int32) // TOP_K
    flat_w = gate.reshape(-1)
    counts = jnp.zeros((N_EXPERTS,), jnp.int32).at[flat_e].add(1)
    padded = (counts + C - 1) // C * C
    pad_end = jnp.cumsum(padded)
    pad_start = pad_end - padded
    start = jnp.cumsum(counts) - counts
    order = jnp.argsort(flat_e, stable=True)
    se = flat_e[order]
    dest = pad_start[se] + jnp.arange(A, dtype=jnp.int32) - start[se]
    row_tok = jnp.zeros((n_blocks * C,), jnp.int32).at[dest].set(flat_tok[order])
    row_w = jnp.zeros((n_blocks * C,), F32).at[dest].set(flat_w[order])
    block_e = jnp.minimum(
        jnp.searchsorted(pad_end, jnp.arange(n_blocks, dtype=jnp.int32) * C, side='right'),
        N_EXPERTS - 1)

    def body(acc, blk):
        tok, w, e = blk
        yb = swiglu(xf[tok], w_gate_up[e], w_down[e]).astype(F32) * w[:, None]
        return acc.at[tok].add(yb), None

    out, _ = lax.scan(body, jnp.zeros((N, D), F32),
                      (row_tok.reshape(n_blocks, C), row_w.reshape(n_blocks, C), block_e))
    return out.astype(xf.dtype)


def moe(x, w_router, router_bias, w_gate_up, w_down, sh_w_gate_up, sh_w_down):
    B, S, D = x.shape
    xf = x.reshape(-1, D)
    N = xf.shape[0]
    scores = jax.nn.sigmoid(jnp.matmul(xf, w_router, preferred_element_type=F32))
    choice = scores + router_bias.astype(F32)
    group_score = lax.top_k(choice.reshape(N, N_GROUPS, N_EXPERTS // N_GROUPS), 2)[0].sum(-1)
    _, g_idx = lax.top_k(group_score, TOPK_GROUPS)
    g_mask = jax.nn.one_hot(g_idx, N_GROUPS, dtype=F32).sum(1) > 0
    e_mask = jnp.repeat(g_mask, N_EXPERTS // N_GROUPS, axis=1)
    _, e_idx = lax.top_k(jnp.where(e_mask, choice, -jnp.inf), TOP_K)
    gate = jnp.take_along_axis(scores, e_idx, axis=1)
    gate = gate / jnp.sum(gate, axis=-1, keepdims=True) * ROUTED_SCALE
    routed = routed_experts(xf, e_idx, gate, w_gate_up, w_down)
    shared = swiglu(xf, sh_w_gate_up, sh_w_down)
    return (routed + shared).reshape(B, S, D)


def setup_inputs(seed: int = 0) -> dict:
    key = jax.random.key(seed)
    ks = jax.random.split(key, 20)
    D, d = D_MODEL, DIFF_HEAD_DIM

    def nrm(k, shape, scale):
        return jax.random.normal(k, shape, F32) * scale

    return {
        'x': nrm(ks[0], (BATCH, SEQ, D), 1.0),
        'ln_g': 1.0 + nrm(ks[1], (DEPTH, 2, D), 0.01),
        'ln_b': nrm(ks[2], (DEPTH, 2, D), 0.01),
        'a_w_qkv': nrm(ks[3], (N_A_LAYERS, D, 3 * D), D ** -0.5),
        'a_lambda': nrm(ks[4], (N_A_LAYERS, 4, d), 0.1),
        'a_subln': 1.0 + nrm(ks[5], (N_A_LAYERS, 2 * d), 0.01),
        'a_w_o': nrm(ks[6], (N_A_LAYERS, D, D), D ** -0.5 * DEEPNORM_BETA),
        'kv_w_a': nrm(ks[7], (D, MLA_KV_RANK + MLA_ROPE_DIM), D ** -0.5),
        'kv_norm': 1.0 + nrm(ks[8], (MLA_KV_RANK,), 0.01),
        'kv_w_b': nrm(ks[9], (MLA_KV_RANK, MLA_HEADS * (MLA_NOPE_DIM + MLA_V_DIM)), MLA_KV_RANK ** -0.5),
        'b_w_dq': nrm(ks[10], (N_B_LAYERS, D, MLA_Q_RANK), D ** -0.5),
        'b_q_norm': 1.0 + nrm(ks[11], (N_B_LAYERS, MLA_Q_RANK), 0.01),
        'b_w_uq': nrm(ks[12], (N_B_LAYERS, MLA_Q_RANK, MLA_HEADS * (MLA_NOPE_DIM + MLA_ROPE_DIM)), MLA_Q_RANK ** -0.5),
        'b_w_o': nrm(ks[13], (N_B_LAYERS, MLA_HEADS * MLA_V_DIM, D), (MLA_HEADS * MLA_V_DIM) ** -0.5 * DEEPNORM_BETA),
        'moe_w_router': nrm(ks[14], (DEPTH, D, N_EXPERTS), D ** -0.5),
        'moe_router_bias': nrm(ks[15], (DEPTH, N_EXPERTS), 0.01),
        'moe_w_gate_up': nrm(ks[16], (DEPTH, N_EXPERTS, D, 2 * EXPERT_DIM), D ** -0.5),
        'moe_w_down': nrm(ks[17], (DEPTH, N_EXPERTS, EXPERT_DIM, D), EXPERT_DIM ** -0.5 * DEEPNORM_BETA),
        'moe_sh_w_gate_up': nrm(ks[18], (DEPTH, D, 2 * SHARED_DIM), D ** -0.5),
        'moe_sh_w_down': nrm(ks[19], (DEPTH, SHARED_DIM, D), SHARED_DIM ** -0.5 * DEEPNORM_BETA),
    }


def reference(x, ln_g, ln_b, a_w_qkv, a_lambda, a_subln, a_w_o, kv_w_a, kv_norm, kv_w_b,
              b_w_dq, b_q_norm, b_w_uq, b_w_o, moe_w_router, moe_router_bias, moe_w_gate_up,
              moe_w_down, moe_sh_w_gate_up, moe_sh_w_down):
    h = x
    k_shared, v_shared = None, None
    for l in range(DEPTH):
        if l < N_A_LAYERS:
            lambda_init = 0.8 - 0.6 * math.exp(-0.3 * l)
            y = diff_attention(h, a_w_qkv[l], a_lambda[l], a_subln[l], a_w_o[l], lambda_init)
        else:
            if l == N_A_LAYERS:
                k_shared, v_shared = mla_shared_kv(h, kv_w_a, kv_norm, kv_w_b)
            j = l - N_A_LAYERS
            y = mla_attention(h, k_shared, v_shared, b_w_dq[j], b_q_norm[j], b_w_uq[j], b_w_o[j])
        h = layer_norm(DEEPNORM_ALPHA * h + y, ln_g[l, 0], ln_b[l, 0])
        y = moe(h, moe_w_router[l], moe_router_bias[l], moe_w_gate_up[l], moe_w_down[l],
                moe_sh_w_gate_up[l], moe_sh_w_down[l])
        h = layer_norm(DEEPNORM_ALPHA * h + y, ln_g[l, 1], ln_b[l, 1])
    return h
```

```python
import functools
import math

import jax
import jax.numpy as jnp
from jax import lax
from jax.experimental import pallas as pl
from jax.experimental.pallas import tpu as pltpu

F32 = jnp.float32
BF16 = jnp.bfloat16

N_GROUPS = 8
TOPK_GROUPS = 4
TOP_K = 8
ROUTED_SCALE = 2.5
ROPE_THETA = 10000.0
LN_EPS = 1e-5
RMS_EPS = 1e-6

LANES = 128
V7X_VMEM_LIMIT = 56 * 1024 * 1024
EXPERT_ROWS = 256
COMBINE_TOKENS = 32
ATTN_TILE = 512


def _params(*sem):
    return pltpu.CompilerParams(dimension_semantics=sem, vmem_limit_bytes=V7X_VMEM_LIMIT)


def _tile(n, pref):
    if n <= pref:
        return n
    t = pref - pref % LANES
    while t >= LANES:
        if n % t == 0:
            return t
        t -= LANES
    return n


def _mm_body(x_ref, w_ref, *rest, epilogue, n_extra):
    extra, o_ref = rest[:n_extra], rest[n_extra]
    acc = jnp.dot(x_ref[...], w_ref[...], preferred_element_type=F32)
    epilogue(acc, o_ref, *extra)


def _matmul(x, w, *, out_dtype, epilogue, extras=(), extra_specs=(), tm=512, tn=1024, name):
    M, K = x.shape
    N = w.shape[1]
    tm, tn = _tile(M, tm), _tile(N, tn)
    return pl.pallas_call(
        functools.partial(_mm_body, epilogue=epilogue, n_extra=len(extras)),
        grid=(N // tn, M // tm),
        in_specs=[pl.BlockSpec((tm, K), lambda j, i: (i, 0)),
                  pl.BlockSpec((K, tn), lambda j, i: (0, j)),
                  *extra_specs],
        out_specs=pl.BlockSpec((tm, tn), lambda j, i: (i, j)),
        out_shape=jax.ShapeDtypeStruct((M, N), out_dtype),
        compiler_params=_params("parallel", "parallel"),
        name=name,
    )(x, w, *extras)


def _ep_plain(acc, o_ref):
    o_ref[...] = acc.astype(o_ref.dtype)


def _ep_sigmoid(acc, o_ref):
    o_ref[...] = jax.nn.sigmoid(acc).astype(o_ref.dtype)


def _ep_rmsnorm(acc, o_ref, g_ref, *, eps):
    ms = jnp.mean(acc * acc, axis=-1, keepdims=True)
    o_ref[...] = (acc * lax.rsqrt(ms + eps) * g_ref[...]).astype(o_ref.dtype)


def _rope_full_group(seg, cos, sin):
    return seg * cos + pltpu.roll(seg, LANES // 2, 1) * sin


def _rope_half_group(seg, c, sa, sb):
    return seg * c + pltpu.roll(seg, 3 * LANES // 4, 1) * sa + pltpu.roll(seg, LANES // 4, 1) * sb


def _ep_qkv(acc, o_ref, tab_ref, *, n_rope_tiles):
    j = pl.program_id(0)

    @pl.when(j < n_rope_tiles)
    def _():
        cos, sin = tab_ref[0], tab_ref[1]
        for g in range(acc.shape[1] // LANES):
            sl = slice(g * LANES, (g + 1) * LANES)
            o_ref[:, sl] = _rope_full_group(acc[:, sl], cos, sin).astype(o_ref.dtype)

    @pl.when(j >= n_rope_tiles)
    def _():
        o_ref[...] = acc.astype(o_ref.dtype)


def _ep_mla_q(acc, o_ref, tab_ref, *, scale):
    c, sa, sb = tab_ref[0], tab_ref[1], tab_ref[2]
    for g in range(acc.shape[1] // LANES):
        sl = slice(g * LANES, (g + 1) * LANES)
        seg = acc[:, sl]
        out = seg * scale if g % 2 == 0 else _rope_half_group(seg, c, sa, sb)
        o_ref[:, sl] = out.astype(o_ref.dtype)


def _ep_kpe(acc, o_ref, tab_ref):
    o_ref[...] = _rope_half_group(acc, tab_ref[0], tab_ref[1], tab_ref[2]).astype(o_ref.dtype)


def _rope_tables_full(S, d, scale):
    inv = ROPE_THETA ** (-jnp.arange(0, d, 2, dtype=F32) / d)
    ang = jnp.arange(S, dtype=F32)[:, None] * inv[None, :]
    cos = jnp.concatenate([jnp.cos(ang), jnp.cos(ang)], axis=-1)
    sin = jnp.concatenate([-jnp.sin(ang), jnp.sin(ang)], axis=-1)
    t = jnp.stack([cos, sin])
    return jnp.stack([t * scale, t])


def _rope_tables_half(S, d, scale):
    inv = ROPE_THETA ** (-jnp.arange(0, d, 2, dtype=F32) / d)
    ang = jnp.arange(S, dtype=F32)[:, None] * inv[None, :]
    cos, sin = jnp.cos(ang), jnp.sin(ang)
    z = jnp.zeros_like(cos)
    pad = jnp.zeros((S, LANES - d), F32)
    c = jnp.concatenate([cos, cos, pad], axis=-1)
    sa = jnp.concatenate([-sin, z, pad], axis=-1)
    sb = jnp.concatenate([z, sin, pad], axis=-1)
    t = jnp.stack([c, sa, sb])
    return jnp.stack([t * scale, t])


def _ln_body(h_ref, y_ref, g_ref, b_ref, of_ref, ob_ref, *, alpha):
    z = alpha * h_ref[...] + y_ref[...]
    mu = jnp.mean(z, axis=-1, keepdims=True)
    zc = z - mu
    var = jnp.mean(zc * zc, axis=-1, keepdims=True)
    o = zc * lax.rsqrt(var + LN_EPS) * g_ref[...] + b_ref[...]
    of_ref[...] = o
    ob_ref[...] = o.astype(BF16)


def _ln_residual(h, y, g, b, alpha):
    N, D = h.shape
    tm = _tile(N, 256)
    row = pl.BlockSpec((tm, D), lambda i: (i, 0))
    vec = pl.BlockSpec((1, D), lambda i: (0, 0))
    return pl.pallas_call(
        functools.partial(_ln_body, alpha=alpha),
        grid=(N // tm,),
        in_specs=[row, row, vec, vec],
        out_specs=[row, row],
        out_shape=[jax.ShapeDtypeStruct((N, D), F32), jax.ShapeDtypeStruct((N, D), BF16)],
        compiler_params=_params("parallel"),
        name="ln_residual",
    )(h, y, g.reshape(1, D), b.reshape(1, D))


def _attn_body(*refs, n_maps, dk, tile, has_kpe, finalize):
    q_ref, k_ref, v_ref = refs[:3]
    pos = 3
    kpe_ref = None
    if has_kpe:
        kpe_ref, pos = refs[3], 4
    fin_refs = refs[pos:-4]
    o_ref, m_sc, l_sc, acc_sc = refs[-4:]
    qi = pl.program_id(2)

    m_sc[...] = jnp.full(m_sc.shape, -jnp.inf, F32)
    l_sc[...] = jnp.zeros(l_sc.shape, F32)
    acc_sc[...] = jnp.zeros(acc_sc.shape, F32)

    def step(ki, masked):
        off = pl.multiple_of(ki * tile, tile)
        k = k_ref[pl.ds(off, tile), :]
        if has_kpe:
            k = jnp.concatenate([k, kpe_ref[pl.ds(off, tile), :]], axis=-1)
        v = v_ref[pl.ds(off, tile), :]
        for m in range(n_maps):
            s = lax.dot_general(q_ref[:, m * dk:(m + 1) * dk], k[:, m * dk:(m + 1) * dk],
                                (((1,), (1,)), ((), ())), preferred_element_type=F32)
            if masked:
                row = lax.broadcasted_iota(jnp.int32, s.shape, 0)
                col = lax.broadcasted_iota(jnp.int32, s.shape, 1)
                s = jnp.where(col <= row, s, -jnp.inf)
            m_old = m_sc[m]
            m_new = jnp.maximum(m_old, jnp.max(s, axis=-1, keepdims=True))
            p = jnp.exp(s - m_new)
            alpha = jnp.exp(m_old - m_new)
            l_sc[m] = alpha * l_sc[m] + jnp.sum(p, axis=-1, keepdims=True)
            acc_sc[m] = alpha * acc_sc[m] + jnp.dot(p.astype(BF16), v, preferred_element_type=F32)
            m_sc[m] = m_new

    def full_step(ki, carry):
        step(ki, False)
        return carry

    lax.fori_loop(0, qi, full_step, 0)
    step(qi, True)
    finalize(o_ref, l_sc, acc_sc, *fin_refs)


def _fin_plain(o_ref, l_sc, acc_sc):
    o_ref[...] = (acc_sc[0] / l_sc[0]).astype(o_ref.dtype)


def _fin_diff(o_ref, l_sc, acc_sc, lam_ref, subln_ref, *, lambda_init):
    lam = lam_ref[...]
    lam_full = (jnp.exp(jnp.sum(lam[0:1] * lam[1:2], axis=-1, keepdims=True))
                - jnp.exp(jnp.sum(lam[2:3] * lam[3:4], axis=-1, keepdims=True)) + lambda_init)
    o = acc_sc[0] / l_sc[0] - lam_full * (acc_sc[1] / l_sc[1])
    ms = jnp.mean(o * o, axis=-1, keepdims=True)
    o = o * lax.rsqrt(ms + LN_EPS) * subln_ref[...] * (1.0 - lambda_init)
    o_ref[...] = o.astype(o_ref.dtype)


def _attention(q, k, v, *, B, S, H, n_maps, dk, dv, q_col0, k_col0, v_col0, k_width,
               kpe=None, finalize, fin_args=(), fin_specs=(), name):
    T = _tile(S, ATTN_TILE)
    nq = S // T
    in_specs = [pl.BlockSpec((T, n_maps * dk), lambda b, h, i: (b * nq + i, q_col0 + h)),
                pl.BlockSpec((S, k_width), lambda b, h, i: (b, k_col0 + h)),
                pl.BlockSpec((S, dv), lambda b, h, i: (b, v_col0 + h))]
    args = [q, k, v]
    if kpe is not None:
        in_specs.append(pl.BlockSpec((S, kpe.shape[1]), lambda b, h, i: (b, 0)))
        args.append(kpe)
    return pl.pallas_call(
        functools.partial(_attn_body, n_maps=n_maps, dk=dk, tile=T, has_kpe=kpe is not None,
                          finalize=finalize),
        grid=(B, H, nq),
        in_specs=in_specs + list(fin_specs),
        out_specs=pl.BlockSpec((T, dv), lambda b, h, i: (b * nq + i, h)),
        out_shape=jax.ShapeDtypeStruct((B * S, H * dv), BF16),
        scratch_shapes=[pltpu.VMEM((n_maps, T, 1), F32), pltpu.VMEM((n_maps, T, 1), F32),
                        pltpu.VMEM((n_maps, T, dv), F32)],
        compiler_params=_params("parallel", "parallel", "parallel"),
        name=name,
    )(*args, *fin_args)


def _shared_body(x_ref, wg_ref, wu_ref, wd_ref, o_ref):
    x = x_ref[...]
    g = jnp.dot(x, wg_ref[...], preferred_element_type=F32)
    u = jnp.dot(x, wu_ref[...], preferred_element_type=F32)
    a = (g * jax.nn.sigmoid(g) * u).astype(BF16)
    o_ref[...] = jnp.dot(a, wd_ref[...], preferred_element_type=F32)


def _shared_expert(xb, wg, wu, wd):
    N, D = xb.shape
    F = wg.shape[1]
    tm = _tile(N, 512)
    full = lambda shape: pl.BlockSpec(shape, lambda i: (0, 0))
    return pl.pallas_call(
        _shared_body,
        grid=(N // tm,),
        in_specs=[pl.BlockSpec((tm, D), lambda i: (i, 0)), full((D, F)), full((D, F)), full((F, D))],
        out_specs=pl.BlockSpec((tm, D), lambda i: (i, 0)),
        out_shape=jax.ShapeDtypeStruct((N, D), F32),
        compiler_params=_params("parallel"),
        name="moe_shared",
    )(xb, wg, wu, wd)


def _expert_body(be_ref, nu_ref, tok_ref, tokn_ref, w_ref, x_hbm, wgu_ref, wd_ref, y_ref,
                 xbuf, sem, wg_sc, wu_sc, wd_sc, *, rows, ffn):
    b = pl.program_id(0)
    nu = nu_ref[0]
    slot = b % 2

    def row_copy(idx_ref, r, s):
        return pltpu.make_async_copy(x_hbm.at[pl.ds(idx_ref[0, 0, r], 1), :],
                                     xbuf.at[s, pl.ds(r, 1), :], sem.at[s])

    def start_gather(idx_ref, s):
        def body(r, carry):
            row_copy(idx_ref, r, s).start()
            return carry
        lax.fori_loop(0, rows, body, 0, unroll=8)

    @pl.when(b == 0)
    def _():
        start_gather(tok_ref, 0)

    @pl.when(b + 1 < nu)
    def _():
        start_gather(tokn_ref, 1 - slot)

    @pl.when(b < nu)
    def _():
        pltpu.make_async_copy(x_hbm.at[pl.ds(0, rows), :], xbuf.at[slot], sem.at[slot]).wait()

        @pl.when((b == 0) | (be_ref[b] != be_ref[jnp.maximum(b - 1, 0)]))
        def _():
            wg_sc[...] = wgu_ref[:, :ffn].astype(BF16)
            wu_sc[...] = wgu_ref[:, ffn:].astype(BF16)
            wd_sc[...] = wd_ref[...].astype(BF16)

        x = xbuf[slot].astype(BF16)
        g = jnp.dot(x, wg_sc[...], preferred_element_type=F32)
        u = jnp.dot(x, wu_sc[...], preferred_element_type=F32)
        a = (g * jax.nn.sigmoid(g) * u).astype(BF16)
        y = jnp.dot(a, wd_sc[...], preferred_element_type=F32)
        y_ref[...] = (y * w_ref[...]).astype(y_ref.dtype)

    @pl.when(b >= nu)
    def _():
        y_ref[...] = jnp.zeros(y_ref.shape, y_ref.dtype)


def _routed_experts(h32, row_tok, row_w, block_e, n_used, w_gate_up, w_down):
    N, D = h32.shape
    E, _, F2 = w_gate_up.shape
    F = F2 // 2
    C = EXPERT_ROWS
    n_blocks = block_e.shape[0]
    tok3 = row_tok.reshape(n_blocks, 1, C)
    smem_blk = lambda fn: pl.BlockSpec((1, 1, C), fn, memory_space=pltpu.SMEM)
    grid_spec = pltpu.PrefetchScalarGridSpec(
        num_scalar_prefetch=2,
        grid=(n_blocks,),
        in_specs=[smem_blk(lambda b, be, nu: (b, 0, 0)),
                  smem_blk(lambda b, be, nu: (jnp.minimum(b + 1, n_blocks - 1), 0, 0)),
                  pl.BlockSpec((C, 1), lambda b, be, nu: (b, 0)),
                  pl.BlockSpec(memory_space=pl.ANY),
                  pl.BlockSpec((None, D, F2), lambda b, be, nu: (be[b], 0, 0)),
                  pl.BlockSpec((None, F, D), lambda b, be, nu: (be[b], 0, 0))],
        out_specs=pl.BlockSpec((C, D), lambda b, be, nu: (b, 0)),
        scratch_shapes=[pltpu.VMEM((2, C, D), F32), pltpu.SemaphoreType.DMA((2,)),
                        pltpu.VMEM((D, F), BF16), pltpu.VMEM((D, F), BF16), pltpu.VMEM((F, D), BF16)],
    )
    return pl.pallas_call(
        functools.partial(_expert_body, rows=C, ffn=F),
        grid_spec=grid_spec,
        out_shape=jax.ShapeDtypeStruct((n_blocks * C, D), F32),
        compiler_params=_params("arbitrary"),
        name="moe_experts",
    )(block_e, n_used, tok3, tok3, row_w.reshape(n_blocks * C, 1), h32, w_gate_up, w_down)


def _combine_body(pos_ref, posn_ref, y_hbm, h_ref, sh_ref, g_ref, b_ref, of_ref, ob_ref,
                  ybuf, sem, *, tokens, top_k, alpha):
    i = pl.program_id(0)
    n = pl.num_programs(0)
    slot = i % 2

    def start_gather(idx_ref, s):
        def body(t, carry):
            for k in range(top_k):
                pltpu.make_async_copy(y_hbm.at[pl.ds(idx_ref[0, 0, t * top_k + k], 1), :],
                                      ybuf.at[s, k, pl.ds(t, 1), :], sem.at[s]).start()
            return carry
        lax.fori_loop(0, tokens, body, 0)

    @pl.when(i == 0)
    def _():
        start_gather(pos_ref, 0)

    @pl.when(i + 1 < n)
    def _():
        start_gather(posn_ref, 1 - slot)

    for k in range(top_k):
        pltpu.make_async_copy(y_hbm.at[pl.ds(0, tokens), :], ybuf.at[slot, k], sem.at[slot]).wait()

    routed = ybuf[slot, 0]
    for k in range(1, top_k):
        routed = routed + ybuf[slot, k]
    z = alpha * h_ref[...] + (routed + sh_ref[...])
    mu = jnp.mean(z, axis=-1, keepdims=True)
    zc = z - mu
    var = jnp.mean(zc * zc, axis=-1, keepdims=True)
    o = zc * lax.rsqrt(var + LN_EPS) * g_ref[...] + b_ref[...]
    of_ref[...] = o
    ob_ref[...] = o.astype(BF16)


def _combine_ln(y_sorted, pos, h32, shared, g, b, alpha):
    N, D = h32.shape
    K = pos.shape[1]
    T = COMBINE_TOKENS
    assert N % T == 0
    n_tiles = N // T
    pos3 = pos.reshape(n_tiles, 1, T * K)
    smem_blk = lambda fn: pl.BlockSpec((1, 1, T * K), fn, memory_space=pltpu.SMEM)
    row = pl.BlockSpec((T, D), lambda i: (i, 0))
    vec = pl.BlockSpec((1, D), lambda i: (0, 0))
    return pl.pallas_call(
        functools.partial(_combine_body, tokens=T, top_k=K, alpha=alpha),
        grid=(n_tiles,),
        in_specs=[smem_blk(lambda i: (i, 0, 0)),
                  smem_blk(lambda i: (jnp.minimum(i + 1, n_tiles - 1), 0, 0)),
                  pl.BlockSpec(memory_space=pl.ANY), row, row, vec, vec],
        out_specs=[row, row],
        out_shape=[jax.ShapeDtypeStruct((N, D), F32), jax.ShapeDtypeStruct((N, D), BF16)],
        scratch_shapes=[pltpu.VMEM((2, K, T, D), F32), pltpu.SemaphoreType.DMA((2,))],
        compiler_params=_params("arbitrary"),
        name="moe_combine_ln",
    )(pos3, pos3, y_sorted, h32, shared, g.reshape(1, D), b.reshape(1, D))


def _route(scores, bias):
    N, E = scores.shape
    C = EXPERT_ROWS
    A = N * TOP_K
    n_blocks = -(-A // C) + E
    choice = scores + bias.astype(F32)
    group_score = lax.top_k(choice.reshape(N, N_GROUPS, E // N_GROUPS), 2)[0].sum(-1)
    _, g_idx = lax.top_k(group_score, TOPK_GROUPS)
    g_mask = jax.nn.one_hot(g_idx, N_GROUPS, dtype=F32).sum(1) > 0
    e_mask = jnp.repeat(g_mask, E // N_GROUPS, axis=1)
    _, e_idx = lax.top_k(jnp.where(e_mask, choice, -jnp.inf), TOP_K)
    gate = jnp.take_along_axis(scores, e_idx, axis=1)
    gate = gate / jnp.sum(gate, axis=-1, keepdims=True) * ROUTED_SCALE

    flat_e = e_idx.reshape(-1).astype(jnp.int32)
    counts = jnp.zeros((E,), jnp.int32).at[flat_e].add(1)
    padded = (counts + C - 1) // C * C
    pad_end = jnp.cumsum(padded)
    pad_start = pad_end - padded
    start = jnp.cumsum(counts) - counts
    order = jnp.argsort(flat_e, stable=True).astype(jnp.int32)
    se = flat_e[order]
    dest = pad_start[se] + jnp.arange(A, dtype=jnp.int32) - start[se]
    row_tok = jnp.zeros((n_blocks * C,), jnp.int32).at[dest].set(order // TOP_K)
    row_w = jnp.zeros((n_blocks * C,), F32).at[dest].set(gate.reshape(-1)[order])
    pos = jnp.zeros((A,), jnp.int32).at[order].set(dest).reshape(N, TOP_K)
    block_e = jnp.minimum(
        jnp.searchsorted(pad_end, jnp.arange(n_blocks, dtype=jnp.int32) * C, side='right'),
        E - 1).astype(jnp.int32)
    n_used = (pad_end[-1:] // C).astype(jnp.int32)
    return row_tok, row_w, block_e, n_used, pos


def _moe_ln(h32, hb, w_router, router_bias, w_gate_up, w_down, sh_w_gate_up, sh_w_down, g, b, alpha):
    F = sh_w_gate_up.shape[1] // 2
    scores = _matmul(hb, w_router.astype(BF16), out_dtype=F32, epilogue=_ep_sigmoid, name="moe_router")
    row_tok, row_w, block_e, n_used, pos = _route(scores, router_bias)
    y_sorted = _routed_experts(h32, row_tok, row_w, block_e, n_used, w_gate_up, w_down)
    shared = _shared_expert(hb, sh_w_gate_up[:, :F].astype(BF16), sh_w_gate_up[:, F:].astype(BF16),
                            sh_w_down.astype(BF16))
    return _combine_ln(y_sorted, pos, h32, shared, g, b, alpha)


def _diff_attention(hb, w_qkv, lam, subln, w_o, lambda_init, B, S):
    N, D = hb.shape
    d = lam.shape[1]
    H = D // (2 * d)
    tn = _tile(D, 1024)
    tm = _tile(S, 512)
    tabs = _rope_tables_full(S, d, d ** -0.5)
    n_rope_tiles = 2 * D // tn
    nsb = S // tm
    qkv = _matmul(
        hb, w_qkv.astype(BF16), out_dtype=BF16, tm=tm, tn=tn,
        epilogue=functools.partial(_ep_qkv, n_rope_tiles=n_rope_tiles),
        extras=(tabs,),
        extra_specs=(pl.BlockSpec((None, 2, tm, d),
                                  lambda j, i: (jnp.minimum(j // (n_rope_tiles // 2), 1), 0, i % nsb, 0)),),
        name="diff_qkv")
    full = lambda shape: pl.BlockSpec(shape, lambda b, h, i: (0, 0))
    o = _attention(qkv, qkv, qkv, B=B, S=S, H=H, n_maps=2, dk=d, dv=2 * d,
                   q_col0=0, k_col0=H, v_col0=2 * H, k_width=2 * d,
                   finalize=functools.partial(_fin_diff, lambda_init=lambda_init),
                   fin_args=(lam, subln.reshape(1, 2 * d)),
                   fin_specs=(full((4, d)), full((1, 2 * d))), name="diff_attn")
    return _matmul(o, w_o.astype(BF16), out_dtype=F32, epilogue=_ep_plain, name="diff_out")


def _mla_shared_kv(hb, w_a, kv_norm, w_b, H, vdim, S):
    N, D = hb.shape
    R = kv_norm.shape[0]
    rope = w_a.shape[1] - R
    hw = w_b.shape[1] // H
    nope = hw - vdim
    assert nope == vdim == LANES and rope <= LANES // 2
    tm = _tile(S, 512)
    nsb = S // tm
    c_kv = _matmul(hb, w_a[:, :R].astype(BF16), out_dtype=BF16, tm=tm, tn=R,
                   epilogue=functools.partial(_ep_rmsnorm, eps=RMS_EPS),
                   extras=(kv_norm.reshape(1, R),),
                   extra_specs=(pl.BlockSpec((1, R), lambda j, i: (0, 0)),), name="mla_kv_a")
    w_pe = jnp.pad(w_a[:, R:], ((0, 0), (0, LANES - rope))).astype(BF16)
    tabs = _rope_tables_half(S, rope, 1.0)
    k_pe = _matmul(hb, w_pe, out_dtype=BF16, tm=tm, tn=LANES, epilogue=_ep_kpe, extras=(tabs,),
                   extra_specs=(pl.BlockSpec((None, 3, tm, LANES), lambda j, i: (1, 0, i % nsb, 0)),),
                   name="mla_k_pe")
    w_b3 = w_b.reshape(R, H, hw)
    w_b2 = jnp.concatenate([w_b3[:, :, :nope].reshape(R, H * nope),
                            w_b3[:, :, nope:].reshape(R, H * (hw - nope))], axis=1).astype(BF16)
    kv = _matmul(c_kv, w_b2, out_dtype=BF16, epilogue=_ep_plain, name="mla_kv_b")
    return kv, k_pe


def _mla_attention(hb, kv, k_pe, w_dq, q_norm, w_uq, w_o, H, B, S):
    N, D = hb.shape
    Qr = q_norm.shape[0]
    hq = w_uq.shape[1] // H
    nope = kv.shape[1] // (2 * H)
    rope = hq - nope
    tm = _tile(S, 512)
    nsb = S // tm
    c_q = _matmul(hb, w_dq.astype(BF16), out_dtype=BF16, tm=tm, tn=Qr,
                  epilogue=functools.partial(_ep_rmsnorm, eps=RMS_EPS),
                  extras=(q_norm.reshape(1, Qr),),
                  extra_specs=(pl.BlockSpec((1, Qr), lambda j, i: (0, 0)),), name="mla_dq")
    scale = float(hq) ** -0.5
    w_q = jnp.pad(w_uq.reshape(Qr, H, hq), ((0, 0), (0, 0), (0, 2 * LANES - hq)))
    w_q = w_q.reshape(Qr, H * 2 * LANES).astype(BF16)
    tabs = _rope_tables_half(S, rope, scale)
    q = _matmul(c_q, w_q, out_dtype=BF16, tm=tm,
                epilogue=functools.partial(_ep_mla_q, scale=scale), extras=(tabs,),
                extra_specs=(pl.BlockSpec((None, 3, tm, LANES), lambda j, i: (0, 0, i % nsb, 0)),),
                name="mla_uq")
    o = _attention(q, kv, kv, B=B, S=S, H=H, n_maps=1, dk=2 * LANES, dv=nope,
                   q_col0=0, k_col0=0, v_col0=H, k_width=nope, kpe=k_pe,
                   finalize=_fin_plain, name="mla_attn")
    return _matmul(o, w_o.astype(BF16), out_dtype=F32, epilogue=_ep_plain, name="mla_out")


def kernel(x, ln_g, ln_b, a_w_qkv, a_lambda, a_subln, a_w_o, kv_w_a, kv_norm, kv_w_b, b_w_dq, b_q_norm, b_w_uq, b_w_o, moe_w_router, moe_router_bias, moe_w_gate_up, moe_w_down, moe_sh_w_gate_up, moe_sh_w_down):
    B, S, D = x.shape
    depth = ln_g.shape[0]
    n_a = a_w_qkv.shape[0]
    alpha = (2 * depth) ** 0.25
    rope_dim = kv_w_a.shape[1] - kv_norm.shape[0]
    mla_heads = (b_w_uq.shape[2] - kv_w_b.shape[1] + b_w_o.shape[1]) // rope_dim
    h32 = x.reshape(B * S, D)
    hb = h32.astype(BF16)
    kv = k_pe = None
    for l in range(depth):
        if l < n_a:
            lambda_init = 0.8 - 0.6 * math.exp(-0.3 * l)
            y = _diff_attention(hb, a_w_qkv[l], a_lambda[l], a_subln[l], a_w_o[l], lambda_init, B, S)
        else:
            if l == n_a:
                kv, k_pe = _mla_shared_kv(hb, kv_w_a, kv_norm, kv_w_b, mla_heads,
                                          b_w_o.shape[1] // mla_heads, S)
            j = l - n_a
            y = _mla_attention(hb, kv, k_pe, b_w_dq[j], b_q_norm[j], b_w_uq[j], b_w_o[j], mla_heads, B, S)
        h32, hb = _ln_residual(h32, y, ln_g[l, 0], ln_b[l, 0], alpha)
        h32, hb = _moe_ln(h32, hb, moe_w_router[l], moe_router_bias[l], moe_w_gate_up[l], moe_w_down[l],
                          moe_sh_w_gate_up[l], moe_sh_w_down[l], ln_g[l, 1], ln_b[l, 1], alpha)
    return h32.reshape(B, S, D)
```

```python
import functools
import math

import jax
import jax.numpy as jnp
from jax import lax
from jax.experimental import pallas as pl
from jax.experimental.pallas import tpu as pltpu

F32 = jnp.float32
BF16 = jnp.bfloat16

N_GROUPS = 8
TOPK_GROUPS = 4
TOP_K = 8
ROUTED_SCALE = 2.5
ROPE_THETA = 10000.0
LN_EPS = 1e-5
RMS_EPS = 1e-6
LOG2E = math.log2(math.e)

LANES = 128
V7X_VMEM_LIMIT = 56 * 1024 * 1024
EXPERT_ROWS = 256
COMBINE_TOKENS = 32
ATTN_TILE = 512


def _params(*sem):
    return pltpu.CompilerParams(dimension_semantics=sem, vmem_limit_bytes=V7X_VMEM_LIMIT)


def _tile(n, pref):
    if n <= pref:
        return n
    t = pref - pref % LANES
    while t >= LANES:
        if n % t == 0:
            return t
        t -= LANES
    return n


def _mm_body(x_ref, w_ref, *rest, epilogue, n_extra):
    extra, o_ref = rest[:n_extra], rest[n_extra]
    acc = jnp.dot(x_ref[...], w_ref[...], preferred_element_type=F32)
    epilogue(acc, o_ref, *extra)


def _matmul(x, w, *, out_dtype, epilogue, extras=(), extra_specs=(), tm=512, tn=1024, name):
    M, K = x.shape
    N = w.shape[1]
    tm, tn = _tile(M, tm), _tile(N, tn)
    return pl.pallas_call(
        functools.partial(_mm_body, epilogue=epilogue, n_extra=len(extras)),
        grid=(N // tn, M // tm),
        in_specs=[pl.BlockSpec((tm, K), lambda j, i: (i, 0)),
                  pl.BlockSpec((K, tn), lambda j, i: (0, j)),
                  *extra_specs],
        out_specs=pl.BlockSpec((tm, tn), lambda j, i: (i, j)),
        out_shape=jax.ShapeDtypeStruct((M, N), out_dtype),
        compiler_params=_params("parallel", "parallel"),
        name=name,
    )(x, w, *extras)


def _ep_plain(acc, o_ref):
    o_ref[...] = acc.astype(o_ref.dtype)


def _ep_rmsnorm(acc, o_ref, g_ref, *, eps):
    ms = jnp.mean(acc * acc, axis=-1, keepdims=True)
    o_ref[...] = (acc * lax.rsqrt(ms + eps) * g_ref[...]).astype(o_ref.dtype)


def _rope_full_group(seg, cos, sin):
    return seg * cos + pltpu.roll(seg, LANES // 2, 1) * sin


def _rope_half_group(seg, c, sa, sb):
    return seg * c + pltpu.roll(seg, 3 * LANES // 4, 1) * sa + pltpu.roll(seg, LANES // 4, 1) * sb


def _ep_qkv(acc, o_ref, tab_ref, *, n_rope_tiles):
    j = pl.program_id(0)

    @pl.when(j < n_rope_tiles)
    def _():
        cos, sin = tab_ref[0], tab_ref[1]
        for g in range(acc.shape[1] // LANES):
            sl = slice(g * LANES, (g + 1) * LANES)
            o_ref[:, sl] = _rope_full_group(acc[:, sl], cos, sin).astype(o_ref.dtype)

    @pl.when(j >= n_rope_tiles)
    def _():
        o_ref[...] = acc.astype(o_ref.dtype)


def _ep_mla_q(acc, o_ref, tab_ref, *, scale):
    c, sa, sb = tab_ref[0], tab_ref[1], tab_ref[2]
    for g in range(acc.shape[1] // LANES):
        sl = slice(g * LANES, (g + 1) * LANES)
        seg = acc[:, sl]
        out = seg * scale if g % 2 == 0 else _rope_half_group(seg, c, sa, sb)
        o_ref[:, sl] = out.astype(o_ref.dtype)


def _ep_kpe(acc, o_ref, tab_ref):
    o_ref[...] = _rope_half_group(acc, tab_ref[0], tab_ref[1], tab_ref[2]).astype(o_ref.dtype)


def _rope_tables_full(S, d, scale):
    inv = ROPE_THETA ** (-jnp.arange(0, d, 2, dtype=F32) / d)
    ang = jnp.arange(S, dtype=F32)[:, None] * inv[None, :]
    cos = jnp.concatenate([jnp.cos(ang), jnp.cos(ang)], axis=-1)
    sin = jnp.concatenate([-jnp.sin(ang), jnp.sin(ang)], axis=-1)
    t = jnp.stack([cos, sin])
    return jnp.stack([t * scale, t])


def _rope_tables_half(S, d, scale):
    inv = ROPE_THETA ** (-jnp.arange(0, d, 2, dtype=F32) / d)
    ang = jnp.arange(S, dtype=F32)[:, None] * inv[None, :]
    cos, sin = jnp.cos(ang), jnp.sin(ang)
    z = jnp.zeros_like(cos)
    pad = jnp.zeros((S, LANES - d), F32)
    c = jnp.concatenate([cos, cos, pad], axis=-1)
    sa = jnp.concatenate([-sin, z, pad], axis=-1)
    sb = jnp.concatenate([z, sin, pad], axis=-1)
    t = jnp.stack([c, sa, sb])
    return jnp.stack([t * scale, t])


def _ln_body(h_ref, y_ref, g_ref, b_ref, of_ref, ob_ref, *, alpha):
    z = alpha * h_ref[...] + y_ref[...]
    mu = jnp.mean(z, axis=-1, keepdims=True)
    zc = z - mu
    var = jnp.mean(zc * zc, axis=-1, keepdims=True)
    o = zc * lax.rsqrt(var + LN_EPS) * g_ref[...] + b_ref[...]
    of_ref[...] = o
    ob_ref[...] = o.astype(BF16)


def _ln_residual(h, y, g, b, alpha):
    N, D = h.shape
    tm = _tile(N, 256)
    row = pl.BlockSpec((tm, D), lambda i: (i, 0))
    vec = pl.BlockSpec((1, D), lambda i: (0, 0))
    return pl.pallas_call(
        functools.partial(_ln_body, alpha=alpha),
        grid=(N // tm,),
        in_specs=[row, row, vec, vec],
        out_specs=[row, row],
        out_shape=[jax.ShapeDtypeStruct((N, D), F32), jax.ShapeDtypeStruct((N, D), BF16)],
        compiler_params=_params("parallel"),
        name="ln_residual",
    )(h, y, g.reshape(1, D), b.reshape(1, D))


def _attn_body(*refs, n_maps, dk, tile, has_kpe, finalize):
    q_ref, k_ref, v_ref = refs[:3]
    pos = 3
    kpe_ref = None
    if has_kpe:
        kpe_ref, pos = refs[3], 4
    fin_refs = refs[pos:-4]
    o_ref, m_sc, l_sc, acc_sc = refs[-4:]
    qi = pl.program_id(2)

    m_sc[...] = jnp.full(m_sc.shape, -jnp.inf, F32)
    l_sc[...] = jnp.zeros(l_sc.shape, F32)
    acc_sc[...] = jnp.zeros(acc_sc.shape, F32)

    def step(ki, masked):
        off = pl.multiple_of(ki * tile, tile)
        k = k_ref[pl.ds(off, tile), :]
        if has_kpe:
            k = jnp.concatenate([k, kpe_ref[pl.ds(off, tile), :]], axis=-1)
        v = v_ref[pl.ds(off, tile), :]
        for m in range(n_maps):
            s = lax.dot_general(q_ref[:, m * dk:(m + 1) * dk], k[:, m * dk:(m + 1) * dk],
                                (((1,), (1,)), ((), ())), preferred_element_type=F32)
            if masked:
                row = lax.broadcasted_iota(jnp.int32, s.shape, 0)
                col = lax.broadcasted_iota(jnp.int32, s.shape, 1)
                s = jnp.where(col <= row, s, -jnp.inf)
            m_old = m_sc[m]
            m_new = jnp.maximum(m_old, jnp.max(s, axis=-1, keepdims=True))
            p = jnp.exp2(s - _lane_tile(m_new, tile))
            alpha = jnp.exp2(m_old - m_new)
            l_sc[m] = alpha * l_sc[m] + jnp.sum(p, axis=-1, keepdims=True)
            acc_sc[m] = (_lane_tile(alpha, acc_sc.shape[2]) * acc_sc[m]
                         + jnp.dot(p.astype(BF16), v, preferred_element_type=F32))
            m_sc[m] = m_new

    def full_step(ki, carry):
        step(ki, False)
        return carry

    lax.fori_loop(0, qi, full_step, 0)
    step(qi, True)
    finalize(o_ref, l_sc, acc_sc, *fin_refs)


def _lane_tile(x, width):
    return jnp.tile(x, (1, width // LANES))


def _fin_plain(o_ref, l_sc, acc_sc):
    o_ref[...] = (acc_sc[0] / _lane_tile(l_sc[0], acc_sc.shape[2])).astype(o_ref.dtype)


def _fin_diff(o_ref, l_sc, acc_sc, lam_ref, subln_ref, *, lambda_init):
    lam = lam_ref[...]
    lam_full = (jnp.exp(jnp.sum(lam[0:1] * lam[1:2], axis=-1, keepdims=True))
                - jnp.exp(jnp.sum(lam[2:3] * lam[3:4], axis=-1, keepdims=True)) + lambda_init)
    dv = acc_sc.shape[2]
    o = acc_sc[0] / _lane_tile(l_sc[0], dv) - lam_full * (acc_sc[1] / _lane_tile(l_sc[1], dv))
    ms = jnp.mean(o * o, axis=-1, keepdims=True)
    o = o * lax.rsqrt(ms + LN_EPS) * subln_ref[...] * (1.0 - lambda_init)
    o_ref[...] = o.astype(o_ref.dtype)


def _attention(q, k, v, *, B, S, H, n_maps, dk, dv, q_col0, k_col0, v_col0, k_width,
               kpe=None, finalize, fin_args=(), fin_specs=(), name):
    T = _tile(S, ATTN_TILE)
    nq = S // T
    in_specs = [pl.BlockSpec((T, n_maps * dk), lambda b, h, i: (b * nq + i, q_col0 + h)),
                pl.BlockSpec((S, k_width), lambda b, h, i: (b, k_col0 + h)),
                pl.BlockSpec((S, dv), lambda b, h, i: (b, v_col0 + h))]
    args = [q, k, v]
    if kpe is not None:
        in_specs.append(pl.BlockSpec((S, kpe.shape[1]), lambda b, h, i: (b, 0)))
        args.append(kpe)
    return pl.pallas_call(
        functools.partial(_attn_body, n_maps=n_maps, dk=dk, tile=T, has_kpe=kpe is not None,
                          finalize=finalize),
        grid=(B, H, nq),
        in_specs=in_specs + list(fin_specs),
        out_specs=pl.BlockSpec((T, dv), lambda b, h, i: (b * nq + i, h)),
        out_shape=jax.ShapeDtypeStruct((B * S, H * dv), BF16),
        scratch_shapes=[pltpu.VMEM((n_maps, T, LANES), F32), pltpu.VMEM((n_maps, T, LANES), F32),
                        pltpu.VMEM((n_maps, T, dv), F32)],
        compiler_params=_params("parallel", "parallel", "parallel"),
        name=name,
    )(*args, *fin_args)


def _shared_body(x_ref, wg_ref, wu_ref, wd_ref, o_ref):
    x = x_ref[...]
    g = jnp.dot(x, wg_ref[...], preferred_element_type=F32)
    u = jnp.dot(x, wu_ref[...], preferred_element_type=F32)
    a = (g * jax.nn.sigmoid(g) * u).astype(BF16)
    o_ref[...] = jnp.dot(a, wd_ref[...], preferred_element_type=F32)


def _shared_expert(xb, wg, wu, wd):
    N, D = xb.shape
    F = wg.shape[1]
    tm = _tile(N, 512)
    full = lambda shape: pl.BlockSpec(shape, lambda i: (0, 0))
    return pl.pallas_call(
        _shared_body,
        grid=(N // tm,),
        in_specs=[pl.BlockSpec((tm, D), lambda i: (i, 0)), full((D, F)), full((D, F)), full((F, D))],
        out_specs=pl.BlockSpec((tm, D), lambda i: (i, 0)),
        out_shape=jax.ShapeDtypeStruct((N, D), F32),
        compiler_params=_params("parallel"),
        name="moe_shared",
    )(xb, wg, wu, wd)


def _expert_body(be_ref, nu_ref, tok_ref, tokn_ref, x_hbm, wgu_ref, wd_ref, y_ref,
                 xbuf, sem, wg_sc, wu_sc, wd_sc, *, rows, ffn, k_chunk):
    b = pl.program_id(0)
    last = pl.num_programs(0) - 1
    nu = nu_ref[0]
    slot = b % 2

    def row_copy(idx_ref, r, s):
        return pltpu.make_async_copy(x_hbm.at[pl.ds(idx_ref[0, 0, r], 1), :],
                                     xbuf.at[s, pl.ds(r, 1), :], sem.at[s])

    def wait_gather(s):
        pltpu.make_async_copy(x_hbm.at[pl.ds(0, rows), :], xbuf.at[s], sem.at[s]).wait()

    @pl.when(b == 0)
    def _():
        def body(r, carry):
            row_copy(tok_ref, r, 0).start()
            return carry
        lax.fori_loop(0, rows, body, 0, unroll=8)

    @pl.when(b <= nu)
    def _():
        wait_gather(slot)

    @pl.when(b < nu)
    def _():
        @pl.when((b == 0) | (be_ref[b] != be_ref[jnp.maximum(b - 1, 0)]))
        def _():
            wg_sc[...] = wgu_ref[:, :ffn].astype(BF16)
            wu_sc[...] = wgu_ref[:, ffn:].astype(BF16)
            wd_sc[...] = wd_ref[...].astype(BF16)

        n_chunks = xbuf.shape[2] // k_chunk
        per_chunk = rows // n_chunks
        g = u = None
        for c in range(n_chunks):
            for r in range(c * per_chunk, (c + 1) * per_chunk):
                row_copy(tokn_ref, r, 1 - slot).start()
            ks = slice(c * k_chunk, (c + 1) * k_chunk)
            xc = xbuf[slot, :, ks].astype(BF16)
            dg = jnp.dot(xc, wg_sc[ks, :], preferred_element_type=F32)
            du = jnp.dot(xc, wu_sc[ks, :], preferred_element_type=F32)
            g = dg if g is None else g + dg
            u = du if u is None else u + du
        a = (g * jax.nn.sigmoid(g) * u).astype(BF16)
        y_ref[...] = jnp.dot(a, wd_sc[...], preferred_element_type=F32)

    @pl.when(b >= nu)
    def _():
        y_ref[...] = jnp.zeros(y_ref.shape, y_ref.dtype)

    @pl.when((b == last) & (b < nu))
    def _():
        wait_gather(1 - slot)


def _routed_experts(h32, row_tok, block_e, n_used, w_gate_up, w_down, layer):
    N, D = h32.shape
    _, E, _, F2 = w_gate_up.shape
    F = F2 // 2
    C = EXPERT_ROWS
    n_blocks = block_e.shape[0]
    k_chunk = _tile(D, 512)
    assert D % k_chunk == 0 and C % (D // k_chunk) == 0
    tok3 = row_tok.reshape(n_blocks, 1, C)
    smem_blk = lambda fn: pl.BlockSpec((1, 1, C), fn, memory_space=pltpu.SMEM)
    grid_spec = pltpu.PrefetchScalarGridSpec(
        num_scalar_prefetch=2,
        grid=(n_blocks,),
        in_specs=[smem_blk(lambda b, be, nu: (b, 0, 0)),
                  smem_blk(lambda b, be, nu: (jnp.minimum(b + 1, n_blocks - 1), 0, 0)),
                  pl.BlockSpec(memory_space=pl.ANY),
                  pl.BlockSpec((None, None, D, F2), lambda b, be, nu: (layer, be[b], 0, 0)),
                  pl.BlockSpec((None, None, F, D), lambda b, be, nu: (layer, be[b], 0, 0))],
        out_specs=pl.BlockSpec((C, D), lambda b, be, nu: (b, 0)),
        scratch_shapes=[pltpu.VMEM((2, C, D), F32), pltpu.SemaphoreType.DMA((2,)),
                        pltpu.VMEM((D, F), BF16), pltpu.VMEM((D, F), BF16), pltpu.VMEM((F, D), BF16)],
    )
    return pl.pallas_call(
        functools.partial(_expert_body, rows=C, ffn=F, k_chunk=k_chunk),
        grid_spec=grid_spec,
        out_shape=jax.ShapeDtypeStruct((n_blocks * C, D), F32),
        compiler_params=_params("arbitrary"),
        name="moe_experts",
    )(block_e, n_used, tok3, tok3, h32, w_gate_up, w_down)


def _combine_body(pos_ref, posn_ref, y_hbm, gate_ref, h_ref, sh_ref, g_ref, b_ref, of_ref, ob_ref,
                  ybuf, sem, *, tokens, top_k, alpha):
    i = pl.program_id(0)
    n = pl.num_programs(0)
    slot = i % 2

    def start_gather(idx_ref, s):
        def body(t, carry):
            for k in range(top_k):
                pltpu.make_async_copy(y_hbm.at[pl.ds(idx_ref[0, 0, t * top_k + k], 1), :],
                                      ybuf.at[s, k, pl.ds(t, 1), :], sem.at[s]).start()
            return carry
        lax.fori_loop(0, tokens, body, 0)

    @pl.when(i == 0)
    def _():
        start_gather(pos_ref, 0)

    @pl.when(i + 1 < n)
    def _():
        start_gather(posn_ref, 1 - slot)

    for k in range(top_k):
        pltpu.make_async_copy(y_hbm.at[pl.ds(0, tokens), :], ybuf.at[slot, k], sem.at[slot]).wait()

    routed = gate_ref[0] * ybuf[slot, 0]
    for k in range(1, top_k):
        routed = routed + gate_ref[k] * ybuf[slot, k]
    z = alpha * h_ref[...] + (routed + sh_ref[...])
    mu = jnp.mean(z, axis=-1, keepdims=True)
    zc = z - mu
    var = jnp.mean(zc * zc, axis=-1, keepdims=True)
    o = zc * lax.rsqrt(var + LN_EPS) * g_ref[...] + b_ref[...]
    of_ref[...] = o
    ob_ref[...] = o.astype(BF16)


def _combine_ln(y_sorted, dest, gates, h32, shared, g, b, alpha):
    N, D = h32.shape
    K = dest.shape[0]
    T = COMBINE_TOKENS
    assert N % T == 0
    n_tiles = N // T
    pos3 = dest.T.reshape(n_tiles, 1, T * K)
    smem_blk = lambda fn: pl.BlockSpec((1, 1, T * K), fn, memory_space=pltpu.SMEM)
    row = pl.BlockSpec((T, D), lambda i: (i, 0))
    vec = pl.BlockSpec((1, D), lambda i: (0, 0))
    return pl.pallas_call(
        functools.partial(_combine_body, tokens=T, top_k=K, alpha=alpha),
        grid=(n_tiles,),
        in_specs=[smem_blk(lambda i: (i, 0, 0)),
                  smem_blk(lambda i: (jnp.minimum(i + 1, n_tiles - 1), 0, 0)),
                  pl.BlockSpec(memory_space=pl.ANY),
                  pl.BlockSpec((K, T, 1), lambda i: (0, i, 0)), row, row, vec, vec],
        out_specs=[row, row],
        out_shape=[jax.ShapeDtypeStruct((N, D), F32), jax.ShapeDtypeStruct((N, D), BF16)],
        scratch_shapes=[pltpu.VMEM((2, K, T, D), F32), pltpu.SemaphoreType.DMA((2,))],
        compiler_params=_params("arbitrary"),
        name="moe_combine_ln",
    )(pos3, pos3, y_sorted, gates.reshape(K, N, 1), h32, shared, g.reshape(1, D), b.reshape(1, D))


def _router_body(x_ref, w_ref, bias_ref, ek_ref, rk_ref, gk_ref, cnt_ref, carry_sc, tri_sc,
                 *, n_groups, topk_groups, top_k, scale):
    i = pl.program_id(0)
    E, T = w_ref.shape[0], x_ref.shape[0]
    G, Eg = n_groups, w_ref.shape[0] // n_groups
    neg = -jnp.inf

    @pl.when(i == 0)
    def _():
        carry_sc[...] = jnp.zeros(carry_sc.shape, F32)
        r = lax.broadcasted_iota(jnp.int32, (T, T), 0)
        c = lax.broadcasted_iota(jnp.int32, (T, T), 1)
        tri_sc[...] = jnp.where(r < c, 1.0, 0.0).astype(BF16)

    logits = lax.dot_general(w_ref[...], x_ref[...], (((1,), (1,)), ((), ())), preferred_element_type=F32)
    scores = jax.nn.sigmoid(logits)
    choice = scores + bias_ref[...]
    sub = lax.broadcasted_iota(jnp.int32, (Eg, T), 0)
    sc_g = [scores[g * Eg:(g + 1) * Eg] for g in range(G)]
    ch_g = [choice[g * Eg:(g + 1) * Eg] for g in range(G)]

    def first_index(hit, idx, size):
        return jnp.min(jnp.where(hit, idx, size), axis=0, keepdims=True)

    gs = []
    for x in ch_g:
        m1 = jnp.max(x, axis=0, keepdims=True)
        first = first_index(x == m1, sub, Eg)
        m2 = jnp.max(jnp.where(sub == first, neg, x), axis=0, keepdims=True)
        gs.append(m1 + m2)
    gs = jnp.concatenate(gs, axis=0)
    gid = lax.broadcasted_iota(jnp.int32, (G, T), 0)
    g_on = jnp.zeros((G, T), F32)
    for _ in range(topk_groups):
        m = jnp.max(gs, axis=0, keepdims=True)
        pick = gid == first_index(gs == m, gid, G)
        g_on = jnp.where(pick, 1.0, g_on)
        gs = jnp.where(pick, neg, gs)
    mk = [jnp.where(g_on[g:g + 1] > 0.0, ch_g[g], neg) for g in range(G)]

    eid = [sub + g * Eg for g in range(G)]
    sel = [jnp.zeros((Eg, T), F32) for _ in range(G)]
    e_k, s_k = [], []
    for _ in range(top_k):
        m = mk[0]
        for g in range(1, G):
            m = jnp.maximum(m, mk[g])
        m = jnp.max(m, axis=0, keepdims=True)
        cand = jnp.where(mk[0] == m, eid[0], E)
        for g in range(1, G):
            cand = jnp.minimum(cand, jnp.where(mk[g] == m, eid[g], E))
        first = jnp.min(cand, axis=0, keepdims=True)
        s = jnp.zeros((Eg, T), F32)
        for g in range(G):
            pick = eid[g] == first
            s = s + jnp.where(pick, sc_g[g], 0.0)
            mk[g] = jnp.where(pick, neg, mk[g])
            sel[g] = jnp.where(pick, 1.0, sel[g])
        e_k.append(first)
        s_k.append(jnp.sum(s, axis=0, keepdims=True))
    denom = s_k[0]
    for s in s_k[1:]:
        denom = denom + s
    gates = [s / denom * scale for s in s_k]

    sel_b = jnp.concatenate(sel, axis=0).astype(BF16)
    rank = (jnp.dot(sel_b, tri_sc[...], preferred_element_type=F32)
            + _lane_tile(carry_sc[...], T))
    total = carry_sc[...] + jnp.dot(sel_b, jnp.ones((T, LANES), BF16), preferred_element_type=F32)
    carry_sc[...] = total
    cnt_ref[...] = total
    r_k = []
    for first in e_k:
        r = jnp.zeros((Eg, T), F32)
        for g in range(G):
            r = r + jnp.where(eid[g] == first, rank[g * Eg:(g + 1) * Eg], 0.0)
        r_k.append(jnp.sum(r, axis=0, keepdims=True))
    ek_ref[...] = jnp.concatenate(e_k, axis=0)
    rk_ref[...] = jnp.concatenate(r_k, axis=0).astype(jnp.int32)
    gk_ref[...] = jnp.concatenate(gates, axis=0)


def _router(hb, w_router, bias):
    N, D = hb.shape
    E = w_router.shape[1]
    T = _tile(N, 512)
    assert E // N_GROUPS == 8 and T % LANES == 0
    kn = pl.BlockSpec((TOP_K, T), lambda i: (0, i))
    ek, rk, gk, cnt = pl.pallas_call(
        functools.partial(_router_body, n_groups=N_GROUPS, topk_groups=TOPK_GROUPS, top_k=TOP_K,
                          scale=ROUTED_SCALE),
        grid=(N // T,),
        in_specs=[pl.BlockSpec((T, D), lambda i: (i, 0)),
                  pl.BlockSpec((E, D), lambda i: (0, 0)),
                  pl.BlockSpec((E, 1), lambda i: (0, 0))],
        out_specs=[kn, kn, kn, pl.BlockSpec((E, LANES), lambda i: (0, 0))],
        out_shape=[jax.ShapeDtypeStruct((TOP_K, N), jnp.int32), jax.ShapeDtypeStruct((TOP_K, N), jnp.int32),
                   jax.ShapeDtypeStruct((TOP_K, N), F32), jax.ShapeDtypeStruct((E, LANES), F32)],
        scratch_shapes=[pltpu.VMEM((E, LANES), F32), pltpu.VMEM((T, T), BF16)],
        compiler_params=_params("arbitrary"),
        name="moe_router",
    )(hb, w_router.T.astype(BF16), bias.astype(F32).reshape(E, 1))
    return ek, rk, gk, cnt[:, 0].astype(jnp.int32)


def _row_layout(ek, rk, counts):
    K, N = ek.shape
    E = counts.shape[0]
    C = EXPERT_ROWS
    n_blocks = -(-(N * K) // C) + E
    padded = (counts + C - 1) // C * C
    pad_end = jnp.cumsum(padded)
    dest = (pad_end - padded)[ek] + rk
    token = jnp.broadcast_to(jnp.arange(N, dtype=jnp.int32)[None, :], (K, N))
    row_tok = jnp.zeros((n_blocks * C,), jnp.int32).at[dest.reshape(-1)].set(
        token.reshape(-1), unique_indices=True)
    block_start = jnp.arange(n_blocks, dtype=jnp.int32) * C
    block_e = jnp.minimum(jnp.sum(pad_end[None, :] <= block_start[:, None], axis=1), E - 1).astype(jnp.int32)
    n_used = (pad_end[-1:] // C).astype(jnp.int32)
    return dest, row_tok, block_e, n_used


def _moe_ln(h32, hb, layer, w_router, router_bias, w_gate_up, w_down, sh_w_gate_up, sh_w_down, g, b, alpha):
    F = sh_w_gate_up.shape[1] // 2
    ek, rk, gates, counts = _router(hb, w_router, router_bias)
    dest, row_tok, block_e, n_used = _row_layout(ek, rk, counts)
    y_sorted = _routed_experts(h32, row_tok, block_e, n_used, w_gate_up, w_down, layer)
    shared = _shared_expert(hb, sh_w_gate_up[:, :F].astype(BF16), sh_w_gate_up[:, F:].astype(BF16),
                            sh_w_down.astype(BF16))
    return _combine_ln(y_sorted, dest, gates, h32, shared, g, b, alpha)


def _diff_attention(hb, w_qkv, lam, subln, w_o, lambda_init, B, S):
    N, D = hb.shape
    d = lam.shape[1]
    H = D // (2 * d)
    tn = _tile(D, 1024)
    tm = _tile(S, 512)
    tabs = _rope_tables_full(S, d, d ** -0.5 * LOG2E)
    n_rope_tiles = 2 * D // tn
    nsb = S // tm
    qkv = _matmul(
        hb, w_qkv.astype(BF16), out_dtype=BF16, tm=tm, tn=tn,
        epilogue=functools.partial(_ep_qkv, n_rope_tiles=n_rope_tiles),
        extras=(tabs,),
        extra_specs=(pl.BlockSpec((None, 2, tm, d),
                                  lambda j, i: (jnp.minimum(j // (n_rope_tiles // 2), 1), 0, i % nsb, 0)),),
        name="diff_qkv")
    full = lambda shape: pl.BlockSpec(shape, lambda b, h, i: (0, 0))
    o = _attention(qkv, qkv, qkv, B=B, S=S, H=H, n_maps=2, dk=d, dv=2 * d,
                   q_col0=0, k_col0=H, v_col0=2 * H, k_width=2 * d,
                   finalize=functools.partial(_fin_diff, lambda_init=lambda_init),
                   fin_args=(lam, subln.reshape(1, 2 * d)),
                   fin_specs=(full((4, d)), full((1, 2 * d))), name="diff_attn")
    return _matmul(o, w_o.astype(BF16), out_dtype=F32, epilogue=_ep_plain, name="diff_out")


def _mla_shared_kv(hb, w_a, kv_norm, w_b, H, vdim, S):
    N, D = hb.shape
    R = kv_norm.shape[0]
    rope = w_a.shape[1] - R
    hw = w_b.shape[1] // H
    nope = hw - vdim
    assert nope == vdim == LANES and rope <= LANES // 2
    tm = _tile(S, 512)
    nsb = S // tm
    c_kv = _matmul(hb, w_a[:, :R].astype(BF16), out_dtype=BF16, tm=tm, tn=R,
                   epilogue=functools.partial(_ep_rmsnorm, eps=RMS_EPS),
                   extras=(kv_norm.reshape(1, R),),
                   extra_specs=(pl.BlockSpec((1, R), lambda j, i: (0, 0)),), name="mla_kv_a")
    w_pe = jnp.pad(w_a[:, R:], ((0, 0), (0, LANES - rope))).astype(BF16)
    tabs = _rope_tables_half(S, rope, 1.0)
    k_pe = _matmul(hb, w_pe, out_dtype=BF16, tm=tm, tn=LANES, epilogue=_ep_kpe, extras=(tabs,),
                   extra_specs=(pl.BlockSpec((None, 3, tm, LANES), lambda j, i: (1, 0, i % nsb, 0)),),
                   name="mla_k_pe")
    w_b3 = w_b.reshape(R, H, hw)
    w_b2 = jnp.concatenate([w_b3[:, :, :nope].reshape(R, H * nope),
                            w_b3[:, :, nope:].reshape(R, H * (hw - nope))], axis=1).astype(BF16)
    kv = _matmul(c_kv, w_b2, out_dtype=BF16, epilogue=_ep_plain, name="mla_kv_b")
    return kv, k_pe


def _mla_attention(hb, kv, k_pe, w_dq, q_norm, w_uq, w_o, H, B, S):
    N, D = hb.shape
    Qr = q_norm.shape[0]
    hq = w_uq.shape[1] // H
    nope = kv.shape[1] // (2 * H)
    rope = hq - nope
    tm = _tile(S, 512)
    nsb = S // tm
    c_q = _matmul(hb, w_dq.astype(BF16), out_dtype=BF16, tm=tm, tn=Qr,
                  epilogue=functools.partial(_ep_rmsnorm, eps=RMS_EPS),
                  extras=(q_norm.reshape(1, Qr),),
                  extra_specs=(pl.BlockSpec((1, Qr), lambda j, i: (0, 0)),), name="mla_dq")
    scale = float(hq) ** -0.5 * LOG2E
    w_q = jnp.pad(w_uq.reshape(Qr, H, hq), ((0, 0), (0, 0), (0, 2 * LANES - hq)))
    w_q = w_q.reshape(Qr, H * 2 * LANES).astype(BF16)
    tabs = _rope_tables_half(S, rope, scale)
    q = _matmul(c_q, w_q, out_dtype=BF16, tm=tm,
                epilogue=functools.partial(_ep_mla_q, scale=scale), extras=(tabs,),
                extra_specs=(pl.BlockSpec((None, 3, tm, LANES), lambda j, i: (0, 0, i % nsb, 0)),),
                name="mla_uq")
    o = _attention(q, kv, kv, B=B, S=S, H=H, n_maps=1, dk=2 * LANES, dv=nope,
                   q_col0=0, k_col0=0, v_col0=H, k_width=nope, kpe=k_pe,
                   finalize=_fin_plain, name="mla_attn")
    return _matmul(o, w_o.astype(BF16), out_dtype=F32, epilogue=_ep_plain, name="mla_out")


def kernel(x, ln_g, ln_b, a_w_qkv, a_lambda, a_subln, a_w_o, kv_w_a, kv_norm, kv_w_b, b_w_dq, b_q_norm, b_w_uq, b_w_o, moe_w_router, moe_router_bias, moe_w_gate_up, moe_w_down, moe_sh_w_gate_up, moe_sh_w_down):
    B, S, D = x.shape
    depth = ln_g.shape[0]
    n_a = a_w_qkv.shape[0]
    alpha = (2 * depth) ** 0.25
    rope_dim = kv_w_a.shape[1] - kv_norm.shape[0]
    mla_heads = (b_w_uq.shape[2] - kv_w_b.shape[1] + b_w_o.shape[1]) // rope_dim
    h32 = x.reshape(B * S, D)
    hb = h32.astype(BF16)
    kv = k_pe = None
    for l in range(depth):
        if l < n_a:
            lambda_init = 0.8 - 0.6 * math.exp(-0.3 * l)
            y = _diff_attention(hb, a_w_qkv[l], a_lambda[l], a_subln[l], a_w_o[l], lambda_init, B, S)
        else:
            if l == n_a:
                kv, k_pe = _mla_shared_kv(hb, kv_w_a, kv_norm, kv_w_b, mla_heads,
                                          b_w_o.shape[1] // mla_heads, S)
            j = l - n_a
            y = _mla_attention(hb, kv, k_pe, b_w_dq[j], b_q_norm[j], b_w_uq[j], b_w_o[j], mla_heads, B, S)
        h32, hb = _ln_residual(h32, y, ln_g[l, 0], ln_b[l, 0], alpha)
        h32, hb = _moe_ln(h32, hb, l, moe_w_router[l], moe_router_bias[l], moe_w_gate_up, moe_w_down,
                          moe_sh_w_gate_up[l], moe_sh_w_down[l], ln_g[l, 1], ln_b[l, 1], alpha)
    return h32.reshape(B, S, D)
```

```python
import functools
import math

import jax
import jax.numpy as jnp
from jax import lax
from jax.experimental import pallas as pl
from jax.experimental.pallas import tpu as pltpu

F32 = jnp.float32
BF16 = jnp.bfloat16

N_GROUPS = 8
TOPK_GROUPS = 4
TOP_K = 8
ROUTED_SCALE = 2.5
ROPE_THETA = 10000.0
LN_EPS = 1e-5
RMS_EPS = 1e-6
LOG2E = math.log2(math.e)

LANES = 128
V7X_VMEM_LIMIT = 56 * 1024 * 1024
EXPERT_ROWS = 512
COMBINE_TOKENS = 64
ATTN_QUERY_TILE = 1024
ATTN_KEY_TILE = 512


def _params(*sem):
    return pltpu.CompilerParams(dimension_semantics=sem, vmem_limit_bytes=V7X_VMEM_LIMIT)


def _tile(n, pref):
    if n <= pref:
        return n
    t = pref - pref % LANES
    while t >= LANES:
        if n % t == 0:
            return t
        t -= LANES
    return n


def _mm_body(x_ref, w_ref, *rest, epilogue, n_extra):
    extra, o_ref = rest[:n_extra], rest[n_extra]
    acc = jnp.dot(x_ref[...], w_ref[...], preferred_element_type=F32)
    epilogue(acc, o_ref, *extra)


def _matmul(x, w, *, out_dtype, epilogue, extras=(), extra_specs=(), tm=512, tn=1024, name):
    M, K = x.shape
    N = w.shape[1]
    tm, tn = _tile(M, tm), _tile(N, tn)
    return pl.pallas_call(
        functools.partial(_mm_body, epilogue=epilogue, n_extra=len(extras)),
        grid=(N // tn, M // tm),
        in_specs=[pl.BlockSpec((tm, K), lambda j, i: (i, 0)),
                  pl.BlockSpec((K, tn), lambda j, i: (0, j)),
                  *extra_specs],
        out_specs=pl.BlockSpec((tm, tn), lambda j, i: (i, j)),
        out_shape=jax.ShapeDtypeStruct((M, N), out_dtype),
        compiler_params=_params("parallel", "parallel"),
        name=name,
    )(x, w, *extras)


def _ep_plain(acc, o_ref):
    o_ref[...] = acc.astype(o_ref.dtype)


def _ep_rmsnorm(acc, o_ref, g_ref, *, eps):
    ms = jnp.mean(acc * acc, axis=-1, keepdims=True)
    o_ref[...] = (acc * lax.rsqrt(ms + eps) * g_ref[...]).astype(o_ref.dtype)


def _rope_full_group(seg, cos, sin):
    return seg * cos + pltpu.roll(seg, LANES // 2, 1) * sin


def _rope_half_group(seg, c, sa, sb):
    return seg * c + pltpu.roll(seg, 3 * LANES // 4, 1) * sa + pltpu.roll(seg, LANES // 4, 1) * sb


def _ep_qkv(acc, o_ref, tab_ref, *, n_rope_tiles):
    j = pl.program_id(0)

    @pl.when(j < n_rope_tiles)
    def _():
        cos, sin = tab_ref[0], tab_ref[1]
        for g in range(acc.shape[1] // LANES):
            sl = slice(g * LANES, (g + 1) * LANES)
            o_ref[:, sl] = _rope_full_group(acc[:, sl], cos, sin).astype(o_ref.dtype)

    @pl.when(j >= n_rope_tiles)
    def _():
        o_ref[...] = acc.astype(o_ref.dtype)


def _ep_mla_q(acc, o_ref, tab_ref, *, scale):
    c, sa, sb = tab_ref[0], tab_ref[1], tab_ref[2]
    for g in range(acc.shape[1] // LANES):
        sl = slice(g * LANES, (g + 1) * LANES)
        seg = acc[:, sl]
        out = seg * scale if g % 2 == 0 else _rope_half_group(seg, c, sa, sb)
        o_ref[:, sl] = out.astype(o_ref.dtype)


def _ep_kpe(acc, o_ref, tab_ref):
    o_ref[...] = _rope_half_group(acc, tab_ref[0], tab_ref[1], tab_ref[2]).astype(o_ref.dtype)


def _rope_tables_full(S, d, scale):
    inv = ROPE_THETA ** (-jnp.arange(0, d, 2, dtype=F32) / d)
    ang = jnp.arange(S, dtype=F32)[:, None] * inv[None, :]
    cos = jnp.concatenate([jnp.cos(ang), jnp.cos(ang)], axis=-1)
    sin = jnp.concatenate([-jnp.sin(ang), jnp.sin(ang)], axis=-1)
    t = jnp.stack([cos, sin])
    return jnp.stack([t * scale, t])


def _rope_tables_half(S, d, scale):
    inv = ROPE_THETA ** (-jnp.arange(0, d, 2, dtype=F32) / d)
    ang = jnp.arange(S, dtype=F32)[:, None] * inv[None, :]
    cos, sin = jnp.cos(ang), jnp.sin(ang)
    z = jnp.zeros_like(cos)
    pad = jnp.zeros((S, LANES - d), F32)
    c = jnp.concatenate([cos, cos, pad], axis=-1)
    sa = jnp.concatenate([-sin, z, pad], axis=-1)
    sb = jnp.concatenate([z, sin, pad], axis=-1)
    t = jnp.stack([c, sa, sb])
    return jnp.stack([t * scale, t])


def _pack_halves(x):
    half = x.shape[1] // 2
    lo = lax.bitcast_convert_type(x[:, :half].astype(BF16).astype(F32), jnp.uint32)
    hi = lax.bitcast_convert_type(x[:, half:].astype(BF16).astype(F32), jnp.uint32)
    return (lo >> 16) | (hi & jnp.uint32(0xFFFF0000))


def _unpack_halves(w):
    lo = lax.bitcast_convert_type(w << 16, F32)
    hi = lax.bitcast_convert_type(w & jnp.uint32(0xFFFF0000), F32)
    return lo, hi


def _ln_body(h_ref, y_ref, g_ref, b_ref, of_ref, ob_ref, op_ref, *, alpha):
    z = alpha * h_ref[...] + y_ref[...]
    mu = jnp.mean(z, axis=-1, keepdims=True)
    zc = z - mu
    var = jnp.mean(zc * zc, axis=-1, keepdims=True)
    o = zc * lax.rsqrt(var + LN_EPS) * g_ref[...] + b_ref[...]
    of_ref[...] = o
    ob_ref[...] = o.astype(BF16)
    op_ref[...] = _pack_halves(o)


def _ln_residual(h, y, g, b, alpha):
    N, D = h.shape
    tm = _tile(N, 256)
    row = pl.BlockSpec((tm, D), lambda i: (i, 0))
    half = pl.BlockSpec((tm, D // 2), lambda i: (i, 0))
    vec = pl.BlockSpec((1, D), lambda i: (0, 0))
    return pl.pallas_call(
        functools.partial(_ln_body, alpha=alpha),
        grid=(N // tm,),
        in_specs=[row, row, vec, vec],
        out_specs=[row, row, half],
        out_shape=[jax.ShapeDtypeStruct((N, D), F32), jax.ShapeDtypeStruct((N, D), BF16),
                   jax.ShapeDtypeStruct((N, D // 2), jnp.uint32)],
        compiler_params=_params("parallel"),
        name="ln_residual",
    )(h, y, g.reshape(1, D), b.reshape(1, D))


def _attn_body(*refs, n_maps, dk, tq, tk, has_kpe, finalize):
    q_ref, k_ref, v_ref = refs[:3]
    pos = 3
    kpe_ref = None
    if has_kpe:
        kpe_ref, pos = refs[3], 4
    fin_refs = refs[pos:-5]
    o_ref, m_sc, l_sc, acc_sc, s_sc = refs[-5:]
    qi = pl.program_id(2)
    ratio = tq // tk

    m_sc[...] = jnp.full(m_sc.shape, -jnp.inf, F32)
    l_sc[...] = jnp.zeros(l_sc.shape, F32)
    acc_sc[...] = jnp.zeros(acc_sc.shape, F32)

    def scores(ki, m, r0):
        off = pl.multiple_of(ki * tk, tk)
        k = k_ref[pl.ds(off, tk), :]
        if has_kpe:
            k = jnp.concatenate([k, kpe_ref[pl.ds(off, tk), :]], axis=-1)
        return lax.dot_general(q_ref[r0:, m * dk:(m + 1) * dk], k[:, m * dk:(m + 1) * dk],
                               (((1,), (1,)), ((), ())), preferred_element_type=F32)

    def accumulate(ki, m, s, r0):
        v = v_ref[pl.ds(pl.multiple_of(ki * tk, tk), tk), :]
        m_old = m_sc[m, r0:]
        m_new = jnp.maximum(m_old, jnp.max(s, axis=-1, keepdims=True))
        p = jnp.exp2(s - _lane_tile(m_new, tk))
        alpha = jnp.exp2(m_old - m_new)
        l_sc[m, r0:] = alpha * l_sc[m, r0:] + jnp.sum(p, axis=-1, keepdims=True)
        acc_sc[m, r0:] = (_lane_tile(alpha, acc_sc.shape[2]) * acc_sc[m, r0:]
                          + jnp.dot(p.astype(BF16), v, preferred_element_type=F32))
        m_sc[m, r0:] = m_new

    def causal(s):
        row = lax.broadcasted_iota(jnp.int32, s.shape, 0)
        col = lax.broadcasted_iota(jnp.int32, s.shape, 1)
        return jnp.where(col <= row, s, -jnp.inf)

    for m in range(n_maps):
        s_sc[m] = scores(0, m, 0)

    def full_step(ki, carry):
        for m in range(n_maps):
            s_next = scores(ki + 1, m, 0)
            accumulate(ki, m, s_sc[m], 0)
            s_sc[m] = s_next
        return carry

    n_full = ratio * qi
    lax.fori_loop(0, n_full, full_step, 0)
    for m in range(n_maps):
        accumulate(n_full, m, causal(s_sc[m]), 0)
    for j in range(1, ratio):
        for m in range(n_maps):
            accumulate(n_full + j, m, causal(scores(n_full + j, m, j * tk)), j * tk)
    finalize(o_ref, l_sc, acc_sc, *fin_refs)


def _lane_tile(x, width):
    return jnp.tile(x, (1, width // LANES))


def _fin_plain(o_ref, l_sc, acc_sc):
    o_ref[...] = (acc_sc[0] / _lane_tile(l_sc[0], acc_sc.shape[2])).astype(o_ref.dtype)


def _fin_diff(o_ref, l_sc, acc_sc, lam_ref, subln_ref, *, lambda_init):
    lam = lam_ref[...]
    lam_full = (jnp.exp(jnp.sum(lam[0:1] * lam[1:2], axis=-1, keepdims=True))
                - jnp.exp(jnp.sum(lam[2:3] * lam[3:4], axis=-1, keepdims=True)) + lambda_init)
    dv = acc_sc.shape[2]
    o = acc_sc[0] / _lane_tile(l_sc[0], dv) - lam_full * (acc_sc[1] / _lane_tile(l_sc[1], dv))
    ms = jnp.mean(o * o, axis=-1, keepdims=True)
    o = o * lax.rsqrt(ms + LN_EPS) * subln_ref[...] * (1.0 - lambda_init)
    o_ref[...] = o.astype(o_ref.dtype)


def _attention(q, k, v, *, B, S, H, n_maps, dk, dv, q_col0, k_col0, v_col0, k_width,
               kpe=None, finalize, fin_args=(), fin_specs=(), name):
    tk = _tile(S, ATTN_KEY_TILE)
    T = _tile(S, ATTN_QUERY_TILE)
    assert T % tk == 0
    nq = S // T
    in_specs = [pl.BlockSpec((T, n_maps * dk), lambda b, h, i: (b * nq + i, q_col0 + h)),
                pl.BlockSpec((S, k_width), lambda b, h, i: (b, k_col0 + h)),
                pl.BlockSpec((S, dv), lambda b, h, i: (b, v_col0 + h))]
    args = [q, k, v]
    if kpe is not None:
        in_specs.append(pl.BlockSpec((S, kpe.shape[1]), lambda b, h, i: (b, 0)))
        args.append(kpe)
    return pl.pallas_call(
        functools.partial(_attn_body, n_maps=n_maps, dk=dk, tq=T, tk=tk, has_kpe=kpe is not None,
                          finalize=finalize),
        grid=(B, H, nq),
        in_specs=in_specs + list(fin_specs),
        out_specs=pl.BlockSpec((T, dv), lambda b, h, i: (b * nq + i, h)),
        out_shape=jax.ShapeDtypeStruct((B * S, H * dv), BF16),
        scratch_shapes=[pltpu.VMEM((n_maps, T, LANES), F32), pltpu.VMEM((n_maps, T, LANES), F32),
                        pltpu.VMEM((n_maps, T, dv), F32), pltpu.VMEM((n_maps, T, tk), F32)],
        compiler_params=_params("parallel", "parallel", "parallel"),
        name=name,
    )(*args, *fin_args)


def _shared_body(x_ref, wg_ref, wu_ref, wd_ref, o_ref):
    x = x_ref[...]
    g = jnp.dot(x, wg_ref[...], preferred_element_type=F32)
    u = jnp.dot(x, wu_ref[...], preferred_element_type=F32)
    a = (g * jax.nn.sigmoid(g) * u).astype(BF16)
    o_ref[...] = jnp.dot(a, wd_ref[...], preferred_element_type=F32)


def _shared_expert(xb, wg, wu, wd):
    N, D = xb.shape
    F = wg.shape[1]
    tm = _tile(N, 512)
    full = lambda shape: pl.BlockSpec(shape, lambda i: (0, 0))
    return pl.pallas_call(
        _shared_body,
        grid=(N // tm,),
        in_specs=[pl.BlockSpec((tm, D), lambda i: (i, 0)), full((D, F)), full((D, F)), full((F, D))],
        out_specs=pl.BlockSpec((tm, D), lambda i: (i, 0)),
        out_shape=jax.ShapeDtypeStruct((N, D), F32),
        compiler_params=_params("parallel"),
        name="moe_shared",
    )(xb, wg, wu, wd)


def _expert_body(be_ref, nu_ref, tok_ref, tokn_ref, x_hbm, wgu_ref, wd_ref, y_ref,
                 xbuf, sem, wg_sc, wu_sc, wd_sc, *, rows, ffn, k_chunk, n_chunk):
    b = pl.program_id(0)
    last = pl.num_programs(0) - 1
    nu = nu_ref[0]
    slot = b % 2

    def row_copy(idx_ref, r, s):
        return pltpu.make_async_copy(x_hbm.at[pl.ds(idx_ref[0, 0, r], 1), :],
                                     xbuf.at[s, pl.ds(r, 1), :], sem.at[s])

    def wait_gather(s):
        pltpu.make_async_copy(x_hbm.at[pl.ds(0, rows), :], xbuf.at[s], sem.at[s]).wait()

    @pl.when(b == 0)
    def _():
        def body(r, carry):
            row_copy(tok_ref, r, 0).start()
            return carry
        lax.fori_loop(0, rows, body, 0, unroll=8)

    @pl.when(b <= nu)
    def _():
        wait_gather(slot)

    @pl.when(b < nu)
    def _():
        @pl.when((b == 0) | (be_ref[b] != be_ref[jnp.maximum(b - 1, 0)]))
        def _():
            wg_sc[...] = wgu_ref[:, :ffn].astype(BF16)
            wu_sc[...] = wgu_ref[:, ffn:].astype(BF16)
            wd_sc[...] = wd_ref[...].astype(BF16)

        half = xbuf.shape[2]
        n_chunks = half // k_chunk
        per_chunk = rows // n_chunks
        g = u = None
        for c in range(n_chunks):
            for r in range(c * per_chunk, (c + 1) * per_chunk):
                row_copy(tokn_ref, r, 1 - slot).start()
            lo, hi = _unpack_halves(xbuf[slot, :, c * k_chunk:(c + 1) * k_chunk])
            for part, k0 in ((lo, c * k_chunk), (hi, half + c * k_chunk)):
                xc = part.astype(BF16)
                dg = jnp.dot(xc, wg_sc[k0:k0 + k_chunk, :], preferred_element_type=F32)
                du = jnp.dot(xc, wu_sc[k0:k0 + k_chunk, :], preferred_element_type=F32)
                g = dg if g is None else g + dg
                u = du if u is None else u + du
        a = (g * jax.nn.sigmoid(g) * u).astype(BF16)
        for c in range(half // n_chunk):
            n0 = c * n_chunk
            y_lo = jnp.dot(a, wd_sc[:, n0:n0 + n_chunk], preferred_element_type=F32)
            y_hi = jnp.dot(a, wd_sc[:, half + n0:half + n0 + n_chunk], preferred_element_type=F32)
            y_ref[:, n0:n0 + n_chunk] = _pack_halves(jnp.concatenate([y_lo, y_hi], axis=1))

    @pl.when(b >= nu)
    def _():
        y_ref[...] = jnp.zeros(y_ref.shape, y_ref.dtype)

    @pl.when((b == last) & (b < nu))
    def _():
        wait_gather(1 - slot)


def _routed_experts(hp, row_tok, block_e, n_used, w_gate_up, w_down, layer):
    N, half = hp.shape
    D = 2 * half
    _, E, _, F2 = w_gate_up.shape
    F = F2 // 2
    C = EXPERT_ROWS
    n_blocks = block_e.shape[0]
    k_chunk = _tile(half, 256)
    n_chunk = _tile(half, 512)
    assert C % (half // k_chunk) == 0
    tok3 = row_tok.reshape(n_blocks, 1, C)
    smem_blk = lambda fn: pl.BlockSpec((1, 1, C), fn, memory_space=pltpu.SMEM)
    grid_spec = pltpu.PrefetchScalarGridSpec(
        num_scalar_prefetch=2,
        grid=(n_blocks,),
        in_specs=[smem_blk(lambda b, be, nu: (b, 0, 0)),
                  smem_blk(lambda b, be, nu: (jnp.minimum(b + 1, n_blocks - 1), 0, 0)),
                  pl.BlockSpec(memory_space=pl.ANY),
                  pl.BlockSpec((None, None, D, F2), lambda b, be, nu: (layer, be[b], 0, 0)),
                  pl.BlockSpec((None, None, F, D), lambda b, be, nu: (layer, be[b], 0, 0))],
        out_specs=pl.BlockSpec((C, half), lambda b, be, nu: (b, 0)),
        scratch_shapes=[pltpu.VMEM((2, C, half), jnp.uint32), pltpu.SemaphoreType.DMA((2,)),
                        pltpu.VMEM((D, F), BF16), pltpu.VMEM((D, F), BF16), pltpu.VMEM((F, D), BF16)],
    )
    return pl.pallas_call(
        functools.partial(_expert_body, rows=C, ffn=F, k_chunk=k_chunk, n_chunk=n_chunk),
        grid_spec=grid_spec,
        out_shape=jax.ShapeDtypeStruct((n_blocks * C, half), jnp.uint32),
        compiler_params=_params("arbitrary"),
        name="moe_experts",
    )(block_e, n_used, tok3, tok3, hp, w_gate_up, w_down)


def _combine_body(pos_ref, posn_ref, y_hbm, gate_ref, h_ref, sh_ref, g_ref, b_ref, of_ref, ob_ref,
                  ybuf, sem, *, tokens, top_k, alpha):
    i = pl.program_id(0)
    n = pl.num_programs(0)
    slot = i % 2

    def start_gather(idx_ref, s):
        def body(t, carry):
            for k in range(top_k):
                pltpu.make_async_copy(y_hbm.at[pl.ds(idx_ref[0, 0, t * top_k + k], 1), :],
                                      ybuf.at[s, k, pl.ds(t, 1), :], sem.at[s]).start()
            return carry
        lax.fori_loop(0, tokens, body, 0)

    @pl.when(i == 0)
    def _():
        start_gather(pos_ref, 0)

    @pl.when(i + 1 < n)
    def _():
        start_gather(posn_ref, 1 - slot)

    for k in range(top_k):
        pltpu.make_async_copy(y_hbm.at[pl.ds(0, tokens), :], ybuf.at[slot, k], sem.at[slot]).wait()

    r_lo = r_hi = None
    for k in range(top_k):
        lo, hi = _unpack_halves(ybuf[slot, k])
        gate = gate_ref[k]
        r_lo = gate * lo if r_lo is None else r_lo + gate * lo
        r_hi = gate * hi if r_hi is None else r_hi + gate * hi
    routed = jnp.concatenate([r_lo, r_hi], axis=1)
    z = alpha * h_ref[...] + (routed + sh_ref[...])
    mu = jnp.mean(z, axis=-1, keepdims=True)
    zc = z - mu
    var = jnp.mean(zc * zc, axis=-1, keepdims=True)
    o = zc * lax.rsqrt(var + LN_EPS) * g_ref[...] + b_ref[...]
    of_ref[...] = o
    ob_ref[...] = o.astype(BF16)


def _combine_ln(y_sorted, dest, gates, h32, shared, g, b, alpha):
    N, D = h32.shape
    K = dest.shape[0]
    T = COMBINE_TOKENS
    assert N % T == 0
    n_tiles = N // T
    pos3 = dest.T.reshape(n_tiles, 1, T * K)
    smem_blk = lambda fn: pl.BlockSpec((1, 1, T * K), fn, memory_space=pltpu.SMEM)
    row = pl.BlockSpec((T, D), lambda i: (i, 0))
    vec = pl.BlockSpec((1, D), lambda i: (0, 0))
    return pl.pallas_call(
        functools.partial(_combine_body, tokens=T, top_k=K, alpha=alpha),
        grid=(n_tiles,),
        in_specs=[smem_blk(lambda i: (i, 0, 0)),
                  smem_blk(lambda i: (jnp.minimum(i + 1, n_tiles - 1), 0, 0)),
                  pl.BlockSpec(memory_space=pl.ANY),
                  pl.BlockSpec((K, T, 1), lambda i: (0, i, 0)), row, row, vec, vec],
        out_specs=[row, row],
        out_shape=[jax.ShapeDtypeStruct((N, D), F32), jax.ShapeDtypeStruct((N, D), BF16)],
        scratch_shapes=[pltpu.VMEM((2, K, T, D // 2), jnp.uint32), pltpu.SemaphoreType.DMA((2,))],
        compiler_params=_params("arbitrary"),
        name="moe_combine_ln",
    )(pos3, pos3, y_sorted, gates.reshape(K, N, 1), h32, shared, g.reshape(1, D), b.reshape(1, D))


def _router_body(x_ref, w_ref, bias_ref, ek_ref, rk_ref, gk_ref, cnt_ref, carry_sc, tri_sc,
                 *, n_groups, topk_groups, top_k, scale):
    i = pl.program_id(0)
    E, T = w_ref.shape[0], x_ref.shape[0]
    G, Eg = n_groups, w_ref.shape[0] // n_groups
    neg = -jnp.inf

    @pl.when(i == 0)
    def _():
        carry_sc[...] = jnp.zeros(carry_sc.shape, F32)
        r = lax.broadcasted_iota(jnp.int32, (T, T), 0)
        c = lax.broadcasted_iota(jnp.int32, (T, T), 1)
        tri_sc[...] = jnp.where(r < c, 1.0, 0.0).astype(BF16)

    logits = lax.dot_general(w_ref[...], x_ref[...], (((1,), (1,)), ((), ())), preferred_element_type=F32)
    scores = jax.nn.sigmoid(logits)
    choice = scores + bias_ref[...]
    sub = lax.broadcasted_iota(jnp.int32, (Eg, T), 0)
    sc_g = [scores[g * Eg:(g + 1) * Eg] for g in range(G)]
    ch_g = [choice[g * Eg:(g + 1) * Eg] for g in range(G)]

    def first_index(hit, idx, size):
        return jnp.min(jnp.where(hit, idx, size), axis=0, keepdims=True)

    gs = []
    for x in ch_g:
        m1 = jnp.max(x, axis=0, keepdims=True)
        first = first_index(x == m1, sub, Eg)
        m2 = jnp.max(jnp.where(sub == first, neg, x), axis=0, keepdims=True)
        gs.append(m1 + m2)
    gs = jnp.concatenate(gs, axis=0)
    gid = lax.broadcasted_iota(jnp.int32, (G, T), 0)
    g_on = jnp.zeros((G, T), F32)
    for _ in range(topk_groups):
        m = jnp.max(gs, axis=0, keepdims=True)
        pick = gid == first_index(gs == m, gid, G)
        g_on = jnp.where(pick, 1.0, g_on)
        gs = jnp.where(pick, neg, gs)
    mk = [jnp.where(g_on[g:g + 1] > 0.0, ch_g[g], neg) for g in range(G)]

    eid = [sub + g * Eg for g in range(G)]
    sel = [jnp.zeros((Eg, T), F32) for _ in range(G)]
    e_k, s_k = [], []
    for _ in range(top_k):
        m = mk[0]
        for g in range(1, G):
            m = jnp.maximum(m, mk[g])
        m = jnp.max(m, axis=0, keepdims=True)
        cand = jnp.where(mk[0] == m, eid[0], E)
        for g in range(1, G):
            cand = jnp.minimum(cand, jnp.where(mk[g] == m, eid[g], E))
        first = jnp.min(cand, axis=0, keepdims=True)
        s = jnp.zeros((Eg, T), F32)
        for g in range(G):
            pick = eid[g] == first
            s = s + jnp.where(pick, sc_g[g], 0.0)
            mk[g] = jnp.where(pick, neg, mk[g])
            sel[g] = jnp.where(pick, 1.0, sel[g])
        e_k.append(first)
        s_k.append(jnp.sum(s, axis=0, keepdims=True))
    denom = s_k[0]
    for s in s_k[1:]:
        denom = denom + s
    gates = [s / denom * scale for s in s_k]

    sel_b = jnp.concatenate(sel, axis=0).astype(BF16)
    rank = (jnp.dot(sel_b, tri_sc[...], preferred_element_type=F32)
            + _lane_tile(carry_sc[...], T))
    total = carry_sc[...] + jnp.dot(sel_b, jnp.ones((T, LANES), BF16), preferred_element_type=F32)
    carry_sc[...] = total
    cnt_ref[...] = total
    r_k = []
    for first in e_k:
        r = jnp.zeros((Eg, T), F32)
        for g in range(G):
            r = r + jnp.where(eid[g] == first, rank[g * Eg:(g + 1) * Eg], 0.0)
        r_k.append(jnp.sum(r, axis=0, keepdims=True))
    ek_ref[...] = jnp.concatenate(e_k, axis=0)
    rk_ref[...] = jnp.concatenate(r_k, axis=0).astype(jnp.int32)
    gk_ref[...] = jnp.concatenate(gates, axis=0)


def _router(hb, w_router, bias):
    N, D = hb.shape
    E = w_router.shape[1]
    T = _tile(N, 512)
    assert E // N_GROUPS == 8 and T % LANES == 0
    kn = pl.BlockSpec((TOP_K, T), lambda i: (0, i))
    ek, rk, gk, cnt = pl.pallas_call(
        functools.partial(_router_body, n_groups=N_GROUPS, topk_groups=TOPK_GROUPS, top_k=TOP_K,
                          scale=ROUTED_SCALE),
        grid=(N // T,),
        in_specs=[pl.BlockSpec((T, D), lambda i: (i, 0)),
                  pl.BlockSpec((E, D), lambda i: (0, 0)),
                  pl.BlockSpec((E, 1), lambda i: (0, 0))],
        out_specs=[kn, kn, kn, pl.BlockSpec((E, LANES), lambda i: (0, 0))],
        out_shape=[jax.ShapeDtypeStruct((TOP_K, N), jnp.int32), jax.ShapeDtypeStruct((TOP_K, N), jnp.int32),
                   jax.ShapeDtypeStruct((TOP_K, N), F32), jax.ShapeDtypeStruct((E, LANES), F32)],
        scratch_shapes=[pltpu.VMEM((E, LANES), F32), pltpu.VMEM((T, T), BF16)],
        compiler_params=_params("arbitrary"),
        name="moe_router",
    )(hb, w_router.T.astype(BF16), bias.astype(F32).reshape(E, 1))
    return ek, rk, gk, cnt[:, 0].astype(jnp.int32)


def _row_layout(ek, rk, counts):
    K, N = ek.shape
    E = counts.shape[0]
    C = EXPERT_ROWS
    n_blocks = -(-(N * K) // C) + E
    padded = (counts + C - 1) // C * C
    pad_end = jnp.cumsum(padded)
    expert = jnp.arange(E, dtype=jnp.int32)[:, None, None]
    dest = jnp.sum(jnp.where(ek[None] == expert, (pad_end - padded)[:, None, None], 0), axis=0) + rk
    token = jnp.broadcast_to(jnp.arange(N, dtype=jnp.int32)[None, :], (K, N))
    row_tok = jnp.zeros((n_blocks * C,), jnp.int32).at[dest.reshape(-1)].set(
        token.reshape(-1), unique_indices=True)
    block_start = jnp.arange(n_blocks, dtype=jnp.int32) * C
    block_e = jnp.minimum(jnp.sum(pad_end[None, :] <= block_start[:, None], axis=1), E - 1).astype(jnp.int32)
    n_used = (pad_end[-1:] // C).astype(jnp.int32)
    return dest, row_tok, block_e, n_used


def _moe_ln(h32, hb, hp, layer, w_router, router_bias, w_gate_up, w_down, sh_w_gate_up, sh_w_down, g, b, alpha):
    F = sh_w_gate_up.shape[1] // 2
    ek, rk, gates, counts = _router(hb, w_router, router_bias)
    dest, row_tok, block_e, n_used = _row_layout(ek, rk, counts)
    y_sorted = _routed_experts(hp, row_tok, block_e, n_used, w_gate_up, w_down, layer)
    shared = _shared_expert(hb, sh_w_gate_up[:, :F].astype(BF16), sh_w_gate_up[:, F:].astype(BF16),
                            sh_w_down.astype(BF16))
    return _combine_ln(y_sorted, dest, gates, h32, shared, g, b, alpha)


def _diff_attention(hb, w_qkv, lam, subln, w_o, lambda_init, B, S):
    N, D = hb.shape
    d = lam.shape[1]
    H = D // (2 * d)
    tn = _tile(D, 1024)
    tm = _tile(S, 512)
    tabs = _rope_tables_full(S, d, d ** -0.5 * LOG2E)
    n_rope_tiles = 2 * D // tn
    nsb = S // tm
    qkv = _matmul(
        hb, w_qkv.astype(BF16), out_dtype=BF16, tm=tm, tn=tn,
        epilogue=functools.partial(_ep_qkv, n_rope_tiles=n_rope_tiles),
        extras=(tabs,),
        extra_specs=(pl.BlockSpec((None, 2, tm, d),
                                  lambda j, i: (jnp.minimum(j // (n_rope_tiles // 2), 1), 0, i % nsb, 0)),),
        name="diff_qkv")
    full = lambda shape: pl.BlockSpec(shape, lambda b, h, i: (0, 0))
    o = _attention(qkv, qkv, qkv, B=B, S=S, H=H, n_maps=2, dk=d, dv=2 * d,
                   q_col0=0, k_col0=H, v_col0=2 * H, k_width=2 * d,
                   finalize=functools.partial(_fin_diff, lambda_init=lambda_init),
                   fin_args=(lam, subln.reshape(1, 2 * d)),
                   fin_specs=(full((4, d)), full((1, 2 * d))), name="diff_attn")
    return _matmul(o, w_o.astype(BF16), out_dtype=F32, epilogue=_ep_plain, name="diff_out")


def _mla_shared_kv(hb, w_a, kv_norm, w_b, H, vdim, S):
    N, D = hb.shape
    R = kv_norm.shape[0]
    rope = w_a.shape[1] - R
    hw = w_b.shape[1] // H
    nope = hw - vdim
    assert nope == vdim == LANES and rope <= LANES // 2
    tm = _tile(S, 512)
    nsb = S // tm
    c_kv = _matmul(hb, w_a[:, :R].astype(BF16), out_dtype=BF16, tm=tm, tn=R,
                   epilogue=functools.partial(_ep_rmsnorm, eps=RMS_EPS),
                   extras=(kv_norm.reshape(1, R),),
                   extra_specs=(pl.BlockSpec((1, R), lambda j, i: (0, 0)),), name="mla_kv_a")
    w_pe = jnp.pad(w_a[:, R:], ((0, 0), (0, LANES - rope))).astype(BF16)
    tabs = _rope_tables_half(S, rope, 1.0)
    k_pe = _matmul(hb, w_pe, out_dtype=BF16, tm=tm, tn=LANES, epilogue=_ep_kpe, extras=(tabs,),
                   extra_specs=(pl.BlockSpec((None, 3, tm, LANES), lambda j, i: (1, 0, i % nsb, 0)),),
                   name="mla_k_pe")
    w_b3 = w_b.reshape(R, H, hw)
    w_b2 = jnp.concatenate([w_b3[:, :, :nope].reshape(R, H * nope),
                            w_b3[:, :, nope:].reshape(R, H * (hw - nope))], axis=1).astype(BF16)
    kv = _matmul(c_kv, w_b2, out_dtype=BF16, epilogue=_ep_plain, name="mla_kv_b")
    return kv, k_pe


def _mla_attention(hb, kv, k_pe, w_dq, q_norm, w_uq, w_o, H, B, S):
    N, D = hb.shape
    Qr = q_norm.shape[0]
    hq = w_uq.shape[1] // H
    nope = kv.shape[1] // (2 * H)
    rope = hq - nope
    tm = _tile(S, 512)
    nsb = S // tm
    c_q = _matmul(hb, w_dq.astype(BF16), out_dtype=BF16, tm=tm, tn=Qr,
                  epilogue=functools.partial(_ep_rmsnorm, eps=RMS_EPS),
                  extras=(q_norm.reshape(1, Qr),),
                  extra_specs=(pl.BlockSpec((1, Qr), lambda j, i: (0, 0)),), name="mla_dq")
    scale = float(hq) ** -0.5 * LOG2E
    w_q = jnp.pad(w_uq.reshape(Qr, H, hq), ((0, 0), (0, 0), (0, 2 * LANES - hq)))
    w_q = w_q.reshape(Qr, H * 2 * LANES).astype(BF16)
    tabs = _rope_tables_half(S, rope, scale)
    q = _matmul(c_q, w_q, out_dtype=BF16, tm=tm,
                epilogue=functools.partial(_ep_mla_q, scale=scale), extras=(tabs,),
                extra_specs=(pl.BlockSpec((None, 3, tm, LANES), lambda j, i: (0, 0, i % nsb, 0)),),
                name="mla_uq")
    o = _attention(q, kv, kv, B=B, S=S, H=H, n_maps=1, dk=2 * LANES, dv=nope,
                   q_col0=0, k_col0=0, v_col0=H, k_width=nope, kpe=k_pe,
                   finalize=_fin_plain, name="mla_attn")
    return _matmul(o, w_o.astype(BF16), out_dtype=F32, epilogue=_ep_plain, name="mla_out")


def kernel(x, ln_g, ln_b, a_w_qkv, a_lambda, a_subln, a_w_o, kv_w_a, kv_norm, kv_w_b, b_w_dq, b_q_norm, b_w_uq, b_w_o, moe_w_router, moe_router_bias, moe_w_gate_up, moe_w_down, moe_sh_w_gate_up, moe_sh_w_down):
    B, S, D = x.shape
    depth = ln_g.shape[0]
    n_a = a_w_qkv.shape[0]
    alpha = (2 * depth) ** 0.25
    rope_dim = kv_w_a.shape[1] - kv_norm.shape[0]
    mla_heads = (b_w_uq.shape[2] - kv_w_b.shape[1] + b_w_o.shape[1]) // rope_dim
    h32 = x.reshape(B * S, D)
    hb = h32.astype(BF16)
    kv = k_pe = None
    for l in range(depth):
        if l < n_a:
            lambda_init = 0.8 - 0.6 * math.exp(-0.3 * l)
            y = _diff_attention(hb, a_w_qkv[l], a_lambda[l], a_subln[l], a_w_o[l], lambda_init, B, S)
        else:
            if l == n_a:
                kv, k_pe = _mla_shared_kv(hb, kv_w_a, kv_norm, kv_w_b, mla_heads,
                                          b_w_o.shape[1] // mla_heads, S)
            j = l - n_a
            y = _mla_attention(hb, kv, k_pe, b_w_dq[j], b_q_norm[j], b_w_uq[j], b_w_o[j], mla_heads, B, S)
        h32, hb, hp = _ln_residual(h32, y, ln_g[l, 0], ln_b[l, 0], alpha)
        h32, hb = _moe_ln(h32, hb, hp, l, moe_w_router[l], moe_router_bias[l], moe_w_gate_up, moe_w_down,
                          moe_sh_w_gate_up[l], moe_sh_w_down[l], ln_g[l, 1], ln_b[l, 1], alpha)
    return h32.reshape(B, S, D)
```

```python
import functools
import math

import jax
import jax.numpy as jnp
from jax import lax
from jax.experimental import pallas as pl
from jax.experimental.pallas import tpu as pltpu

F32 = jnp.float32
BF16 = jnp.bfloat16

N_GROUPS = 8
TOPK_GROUPS = 4
TOP_K = 8
ROUTED_SCALE = 2.5
ROPE_THETA = 10000.0
LN_EPS = 1e-5
RMS_EPS = 1e-6
LOG2E = math.log2(math.e)

LANES = 128
V7X_VMEM_LIMIT = 56 * 1024 * 1024
EXPERT_ROWS = 512
COMBINE_TOKENS = 64
ATTN_QUERY_TILE = 1024
ATTN_KEY_TILE = 512


def _params(*sem):
    return pltpu.CompilerParams(dimension_semantics=sem, vmem_limit_bytes=V7X_VMEM_LIMIT)


def _tile(n, pref):
    if n <= pref:
        return n
    t = pref - pref % LANES
    while t >= LANES:
        if n % t == 0:
            return t
        t -= LANES
    return n


def _mm_body(x_ref, w_ref, *rest, epilogue, n_extra):
    extra, o_ref = rest[:n_extra], rest[n_extra]
    acc = jnp.dot(x_ref[...], w_ref[...], preferred_element_type=F32)
    epilogue(acc, o_ref, *extra)


def _matmul(x, w, *, out_dtype, epilogue, extras=(), extra_specs=(), tm=512, tn=1024, name):
    M, K = x.shape
    N = w.shape[1]
    tm, tn = _tile(M, tm), _tile(N, tn)
    return pl.pallas_call(
        functools.partial(_mm_body, epilogue=epilogue, n_extra=len(extras)),
        grid=(N // tn, M // tm),
        in_specs=[pl.BlockSpec((tm, K), lambda j, i: (i, 0)),
                  pl.BlockSpec((K, tn), lambda j, i: (0, j)),
                  *extra_specs],
        out_specs=pl.BlockSpec((tm, tn), lambda j, i: (i, j)),
        out_shape=jax.ShapeDtypeStruct((M, N), out_dtype),
        compiler_params=_params("parallel", "parallel"),
        name=name,
    )(x, w, *extras)


def _ep_plain(acc, o_ref):
    o_ref[...] = acc.astype(o_ref.dtype)


def _ep_rmsnorm(acc, o_ref, g_ref, *, eps):
    ms = jnp.mean(acc * acc, axis=-1, keepdims=True)
    o_ref[...] = (acc * lax.rsqrt(ms + eps) * g_ref[...]).astype(o_ref.dtype)


def _rope_full_group(seg, cos, sin):
    return seg * cos + pltpu.roll(seg, LANES // 2, 1) * sin


def _rope_half_group(seg, c, sa, sb):
    return seg * c + pltpu.roll(seg, 3 * LANES // 4, 1) * sa + pltpu.roll(seg, LANES // 4, 1) * sb


def _ep_qkv(acc, o_ref, tab_ref, *, n_rope_tiles):
    j = pl.program_id(0)

    @pl.when(j < n_rope_tiles)
    def _():
        cos, sin = tab_ref[0], tab_ref[1]
        for g in range(acc.shape[1] // LANES):
            sl = slice(g * LANES, (g + 1) * LANES)
            o_ref[:, sl] = _rope_full_group(acc[:, sl], cos, sin).astype(o_ref.dtype)

    @pl.when(j >= n_rope_tiles)
    def _():
        o_ref[...] = acc.astype(o_ref.dtype)


def _ep_mla_q(acc, o_ref, tab_ref, *, scale):
    c, sa, sb = tab_ref[0], tab_ref[1], tab_ref[2]
    for g in range(acc.shape[1] // LANES):
        sl = slice(g * LANES, (g + 1) * LANES)
        seg = acc[:, sl]
        out = seg * scale if g % 2 == 0 else _rope_half_group(seg, c, sa, sb)
        o_ref[:, sl] = out.astype(o_ref.dtype)


def _ep_kpe(acc, o_ref, tab_ref):
    o_ref[...] = _rope_half_group(acc, tab_ref[0], tab_ref[1], tab_ref[2]).astype(o_ref.dtype)


def _rope_tables_full(S, d, scale):
    inv = ROPE_THETA ** (-jnp.arange(0, d, 2, dtype=F32) / d)
    ang = jnp.arange(S, dtype=F32)[:, None] * inv[None, :]
    cos = jnp.concatenate([jnp.cos(ang), jnp.cos(ang)], axis=-1)
    sin = jnp.concatenate([-jnp.sin(ang), jnp.sin(ang)], axis=-1)
    t = jnp.stack([cos, sin])
    return jnp.stack([t * scale, t])


def _rope_tables_half(S, d, scale):
    inv = ROPE_THETA ** (-jnp.arange(0, d, 2, dtype=F32) / d)
    ang = jnp.arange(S, dtype=F32)[:, None] * inv[None, :]
    cos, sin = jnp.cos(ang), jnp.sin(ang)
    z = jnp.zeros_like(cos)
    pad = jnp.zeros((S, LANES - d), F32)
    c = jnp.concatenate([cos, cos, pad], axis=-1)
    sa = jnp.concatenate([-sin, z, pad], axis=-1)
    sb = jnp.concatenate([z, sin, pad], axis=-1)
    t = jnp.stack([c, sa, sb])
    return jnp.stack([t * scale, t])


def _pack_halves(x):
    half = x.shape[1] // 2
    lo = lax.bitcast_convert_type(x[:, :half].astype(BF16).astype(F32), jnp.uint32)
    hi = lax.bitcast_convert_type(x[:, half:].astype(BF16).astype(F32), jnp.uint32)
    return (lo >> 16) | (hi & jnp.uint32(0xFFFF0000))


def _unpack_halves(w):
    lo = lax.bitcast_convert_type(w << 16, F32)
    hi = lax.bitcast_convert_type(w & jnp.uint32(0xFFFF0000), F32)
    return lo, hi


def _ln_body(h_ref, y_ref, g_ref, b_ref, of_ref, ob_ref, op_ref, *, alpha):
    z = alpha * h_ref[...] + y_ref[...]
    mu = jnp.mean(z, axis=-1, keepdims=True)
    zc = z - mu
    var = jnp.mean(zc * zc, axis=-1, keepdims=True)
    o = zc * lax.rsqrt(var + LN_EPS) * g_ref[...] + b_ref[...]
    of_ref[...] = o
    ob_ref[...] = o.astype(BF16)
    op_ref[...] = _pack_halves(o)


def _ln_residual(h, y, g, b, alpha):
    N, D = h.shape
    tm = _tile(N, 256)
    row = pl.BlockSpec((tm, D), lambda i: (i, 0))
    half = pl.BlockSpec((tm, D // 2), lambda i: (i, 0))
    vec = pl.BlockSpec((1, D), lambda i: (0, 0))
    return pl.pallas_call(
        functools.partial(_ln_body, alpha=alpha),
        grid=(N // tm,),
        in_specs=[row, row, vec, vec],
        out_specs=[row, row, half],
        out_shape=[jax.ShapeDtypeStruct((N, D), F32), jax.ShapeDtypeStruct((N, D), BF16),
                   jax.ShapeDtypeStruct((N, D // 2), jnp.uint32)],
        compiler_params=_params("parallel"),
        name="ln_residual",
    )(h, y, g.reshape(1, D), b.reshape(1, D))


def _attn_body(*refs, n_maps, dk, tq, tk, has_kpe, finalize):
    q_ref, k_ref, v_ref = refs[:3]
    pos = 3
    kpe_ref = None
    if has_kpe:
        kpe_ref, pos = refs[3], 4
    fin_refs = refs[pos:-5]
    o_ref, m_sc, l_sc, acc_sc, s_sc = refs[-5:]
    qi = pl.program_id(2)
    ratio = tq // tk

    m_sc[...] = jnp.full(m_sc.shape, -jnp.inf, F32)
    l_sc[...] = jnp.zeros(l_sc.shape, F32)
    acc_sc[...] = jnp.zeros(acc_sc.shape, F32)

    def scores(ki, m, r0):
        off = pl.multiple_of(ki * tk, tk)
        k = k_ref[pl.ds(off, tk), :]
        if has_kpe:
            k = jnp.concatenate([k, kpe_ref[pl.ds(off, tk), :]], axis=-1)
        return lax.dot_general(q_ref[r0:, m * dk:(m + 1) * dk], k[:, m * dk:(m + 1) * dk],
                               (((1,), (1,)), ((), ())), preferred_element_type=F32)

    def accumulate(ki, m, s, r0):
        v = v_ref[pl.ds(pl.multiple_of(ki * tk, tk), tk), :]
        m_old = m_sc[m, r0:]
        m_new = jnp.maximum(m_old, jnp.max(s, axis=-1, keepdims=True))
        p = jnp.exp2(s - _lane_tile(m_new, tk))
        alpha = jnp.exp2(m_old - m_new)
        l_sc[m, r0:] = alpha * l_sc[m, r0:] + jnp.sum(p, axis=-1, keepdims=True)
        acc_sc[m, r0:] = (_lane_tile(alpha, acc_sc.shape[2]) * acc_sc[m, r0:]
                          + jnp.dot(p.astype(BF16), v, preferred_element_type=F32))
        m_sc[m, r0:] = m_new

    def causal(s):
        row = lax.broadcasted_iota(jnp.int32, s.shape, 0)
        col = lax.broadcasted_iota(jnp.int32, s.shape, 1)
        return jnp.where(col <= row, s, -jnp.inf)

    for m in range(n_maps):
        s_sc[m] = scores(0, m, 0)

    def full_step(ki, carry):
        for m in range(n_maps):
            s_next = scores(ki + 1, m, 0)
            accumulate(ki, m, s_sc[m], 0)
            s_sc[m] = s_next
        return carry

    n_full = ratio * qi
    lax.fori_loop(0, n_full, full_step, 0)
    for m in range(n_maps):
        accumulate(n_full, m, causal(s_sc[m]), 0)
    for j in range(1, ratio):
        for m in range(n_maps):
            accumulate(n_full + j, m, causal(scores(n_full + j, m, j * tk)), j * tk)
    finalize(o_ref, l_sc, acc_sc, *fin_refs)


def _lane_tile(x, width):
    return jnp.tile(x, (1, width // LANES))


def _fin_plain(o_ref, l_sc, acc_sc):
    o_ref[...] = (acc_sc[0] / _lane_tile(l_sc[0], acc_sc.shape[2])).astype(o_ref.dtype)


def _fin_diff(o_ref, l_sc, acc_sc, lam_ref, subln_ref, *, lambda_init):
    lam = lam_ref[...]
    lam_full = (jnp.exp(jnp.sum(lam[0:1] * lam[1:2], axis=-1, keepdims=True))
                - jnp.exp(jnp.sum(lam[2:3] * lam[3:4], axis=-1, keepdims=True)) + lambda_init)
    dv = acc_sc.shape[2]
    o = acc_sc[0] / _lane_tile(l_sc[0], dv) - lam_full * (acc_sc[1] / _lane_tile(l_sc[1], dv))
    ms = jnp.mean(o * o, axis=-1, keepdims=True)
    o = o * lax.rsqrt(ms + LN_EPS) * subln_ref[...] * (1.0 - lambda_init)
    o_ref[...] = o.astype(o_ref.dtype)


def _attention(q, k, v, *, B, S, H, n_maps, dk, dv, q_col0, k_col0, v_col0, k_width,
               kpe=None, finalize, fin_args=(), fin_specs=(), name):
    tk = _tile(S, ATTN_KEY_TILE)
    T = _tile(S, ATTN_QUERY_TILE)
    assert T % tk == 0
    nq = S // T
    in_specs = [pl.BlockSpec((T, n_maps * dk), lambda b, h, i: (b * nq + i, q_col0 + h)),
                pl.BlockSpec((S, k_width), lambda b, h, i: (b, k_col0 + h)),
                pl.BlockSpec((S, dv), lambda b, h, i: (b, v_col0 + h))]
    args = [q, k, v]
    if kpe is not None:
        in_specs.append(pl.BlockSpec((S, kpe.shape[1]), lambda b, h, i: (b, 0)))
        args.append(kpe)
    return pl.pallas_call(
        functools.partial(_attn_body, n_maps=n_maps, dk=dk, tq=T, tk=tk, has_kpe=kpe is not None,
                          finalize=finalize),
        grid=(B, H, nq),
        in_specs=in_specs + list(fin_specs),
        out_specs=pl.BlockSpec((T, dv), lambda b, h, i: (b * nq + i, h)),
        out_shape=jax.ShapeDtypeStruct((B * S, H * dv), BF16),
        scratch_shapes=[pltpu.VMEM((n_maps, T, LANES), F32), pltpu.VMEM((n_maps, T, LANES), F32),
                        pltpu.VMEM((n_maps, T, dv), F32), pltpu.VMEM((n_maps, T, tk), F32)],
        compiler_params=_params("parallel", "parallel", "parallel"),
        name=name,
    )(*args, *fin_args)


def _shared_body(x_ref, wg_ref, wu_ref, wd_ref, o_ref):
    x = x_ref[...]
    g = jnp.dot(x, wg_ref[...], preferred_element_type=F32)
    u = jnp.dot(x, wu_ref[...], preferred_element_type=F32)
    a = (g * jax.nn.sigmoid(g) * u).astype(BF16)
    o_ref[...] = jnp.dot(a, wd_ref[...], preferred_element_type=F32)


def _shared_expert(xb, wg, wu, wd):
    N, D = xb.shape
    F = wg.shape[1]
    tm = _tile(N, 512)
    full = lambda shape: pl.BlockSpec(shape, lambda i: (0, 0))
    return pl.pallas_call(
        _shared_body,
        grid=(N // tm,),
        in_specs=[pl.BlockSpec((tm, D), lambda i: (i, 0)), full((D, F)), full((D, F)), full((F, D))],
        out_specs=pl.BlockSpec((tm, D), lambda i: (i, 0)),
        out_shape=jax.ShapeDtypeStruct((N, D), F32),
        compiler_params=_params("parallel"),
        name="moe_shared",
    )(xb, wg, wu, wd)


def _expert_body(be_ref, nu_ref, tok_ref, tokn_ref, x_hbm, wgu_ref, wd_ref, y_ref,
                 xbuf, sem, wg_sc, wu_sc, wd_sc, *, rows, ffn, k_chunk, n_chunk):
    b = pl.program_id(0)
    last = pl.num_programs(0) - 1
    nu = nu_ref[0]
    slot = b % 2

    def row_copy(idx_ref, r, s):
        return pltpu.make_async_copy(x_hbm.at[pl.ds(idx_ref[0, 0, r], 1), :],
                                     xbuf.at[s, pl.ds(r, 1), :], sem.at[s])

    def wait_gather(s):
        pltpu.make_async_copy(x_hbm.at[pl.ds(0, rows), :], xbuf.at[s], sem.at[s]).wait()

    @pl.when(b == 0)
    def _():
        def body(r, carry):
            row_copy(tok_ref, r, 0).start()
            return carry
        lax.fori_loop(0, rows, body, 0, unroll=8)

    @pl.when(b <= nu)
    def _():
        wait_gather(slot)

    @pl.when(b < nu)
    def _():
        @pl.when((b == 0) | (be_ref[b] != be_ref[jnp.maximum(b - 1, 0)]))
        def _():
            wg_sc[...] = wgu_ref[:, :ffn].astype(BF16)
            wu_sc[...] = wgu_ref[:, ffn:].astype(BF16)
            wd_sc[...] = wd_ref[...].astype(BF16)

        half = xbuf.shape[2]
        n_chunks = half // k_chunk
        n_down = half // n_chunk
        work = [2 * k_chunk * 2 * ffn] * n_chunks + [2 * n_chunk * ffn] * n_down
        edges = [rows * sum(work[:i]) // sum(work) for i in range(len(work) + 1)]

        def start_next_rows(phase):
            for r in range(edges[phase], edges[phase + 1]):
                row_copy(tokn_ref, r, 1 - slot).start(priority=r % 2)

        g = u = None
        for c in range(n_chunks):
            start_next_rows(c)
            lo, hi = _unpack_halves(xbuf[slot, :, c * k_chunk:(c + 1) * k_chunk])
            for part, k0 in ((lo, c * k_chunk), (hi, half + c * k_chunk)):
                xc = part.astype(BF16)
                dg = jnp.dot(xc, wg_sc[k0:k0 + k_chunk, :], preferred_element_type=F32)
                du = jnp.dot(xc, wu_sc[k0:k0 + k_chunk, :], preferred_element_type=F32)
                g = dg if g is None else g + dg
                u = du if u is None else u + du
        a = (g * jax.nn.sigmoid(g) * u).astype(BF16)
        for c in range(n_down):
            start_next_rows(n_chunks + c)
            n0 = c * n_chunk
            y_lo = jnp.dot(a, wd_sc[:, n0:n0 + n_chunk], preferred_element_type=F32)
            y_hi = jnp.dot(a, wd_sc[:, half + n0:half + n0 + n_chunk], preferred_element_type=F32)
            y_ref[:, n0:n0 + n_chunk] = _pack_halves(jnp.concatenate([y_lo, y_hi], axis=1))

    @pl.when(b >= nu)
    def _():
        y_ref[...] = jnp.zeros(y_ref.shape, y_ref.dtype)

    @pl.when((b == last) & (b < nu))
    def _():
        wait_gather(1 - slot)


def _routed_experts(hp, row_tok, block_e, n_used, w_gate_up, w_down, layer):
    N, half = hp.shape
    D = 2 * half
    _, E, _, F2 = w_gate_up.shape
    F = F2 // 2
    C = EXPERT_ROWS
    n_blocks = block_e.shape[0]
    k_chunk = _tile(half, 256)
    n_chunk = _tile(half, 512)
    assert C % (half // k_chunk) == 0
    tok3 = row_tok.reshape(n_blocks, 1, C)
    smem_blk = lambda fn: pl.BlockSpec((1, 1, C), fn, memory_space=pltpu.SMEM)
    grid_spec = pltpu.PrefetchScalarGridSpec(
        num_scalar_prefetch=2,
        grid=(n_blocks,),
        in_specs=[smem_blk(lambda b, be, nu: (b, 0, 0)),
                  smem_blk(lambda b, be, nu: (jnp.minimum(b + 1, n_blocks - 1), 0, 0)),
                  pl.BlockSpec(memory_space=pl.ANY),
                  pl.BlockSpec((None, None, D, F2), lambda b, be, nu: (layer, be[b], 0, 0)),
                  pl.BlockSpec((None, None, F, D), lambda b, be, nu: (layer, be[b], 0, 0))],
        out_specs=pl.BlockSpec((C, half), lambda b, be, nu: (b, 0)),
        scratch_shapes=[pltpu.VMEM((2, C, half), jnp.uint32), pltpu.SemaphoreType.DMA((2,)),
                        pltpu.VMEM((D, F), BF16), pltpu.VMEM((D, F), BF16), pltpu.VMEM((F, D), BF16)],
    )
    return pl.pallas_call(
        functools.partial(_expert_body, rows=C, ffn=F, k_chunk=k_chunk, n_chunk=n_chunk),
        grid_spec=grid_spec,
        out_shape=jax.ShapeDtypeStruct((n_blocks * C, half), jnp.uint32),
        compiler_params=_params("arbitrary"),
        name="moe_experts",
    )(block_e, n_used, tok3, tok3, hp, w_gate_up, w_down)


def _combine_body(pos_ref, posn_ref, y_hbm, gate_ref, h_ref, sh_ref, g_ref, b_ref, of_ref, ob_ref,
                  ybuf, sem, *, tokens, top_k, alpha):
    i = pl.program_id(0)
    last = pl.num_programs(0) - 1
    slot = i % 2

    def row_copy(idx_ref, t, k, s):
        return pltpu.make_async_copy(y_hbm.at[pl.ds(idx_ref[0, 0, t * top_k + k], 1), :],
                                     ybuf.at[s, k, pl.ds(t, 1), :], sem.at[s])

    def wait_gather(s):
        for k in range(top_k):
            pltpu.make_async_copy(y_hbm.at[pl.ds(0, tokens), :], ybuf.at[s, k], sem.at[s]).wait()

    @pl.when(i == 0)
    def _():
        def body(t, carry):
            for k in range(top_k):
                row_copy(pos_ref, t, k, 0).start()
            return carry
        lax.fori_loop(0, tokens, body, 0)

    wait_gather(slot)

    r_lo = r_hi = None
    for k in range(top_k):
        for t in range(tokens):
            row_copy(posn_ref, t, k, 1 - slot).start(priority=t % 2)
        lo, hi = _unpack_halves(ybuf[slot, k])
        gate = gate_ref[k]
        r_lo = gate * lo if r_lo is None else r_lo + gate * lo
        r_hi = gate * hi if r_hi is None else r_hi + gate * hi

    @pl.when(i == last)
    def _():
        wait_gather(1 - slot)

    routed = jnp.concatenate([r_lo, r_hi], axis=1)
    z = alpha * h_ref[...] + (routed + sh_ref[...])
    mu = jnp.mean(z, axis=-1, keepdims=True)
    zc = z - mu
    var = jnp.mean(zc * zc, axis=-1, keepdims=True)
    o = zc * lax.rsqrt(var + LN_EPS) * g_ref[...] + b_ref[...]
    of_ref[...] = o
    ob_ref[...] = o.astype(BF16)


def _combine_ln(y_sorted, dest, gates, h32, shared, g, b, alpha):
    N, D = h32.shape
    K = dest.shape[0]
    T = COMBINE_TOKENS
    assert N % T == 0
    n_tiles = N // T
    pos3 = dest.T.reshape(n_tiles, 1, T * K)
    smem_blk = lambda fn: pl.BlockSpec((1, 1, T * K), fn, memory_space=pltpu.SMEM)
    row = pl.BlockSpec((T, D), lambda i: (i, 0))
    vec = pl.BlockSpec((1, D), lambda i: (0, 0))
    return pl.pallas_call(
        functools.partial(_combine_body, tokens=T, top_k=K, alpha=alpha),
        grid=(n_tiles,),
        in_specs=[smem_blk(lambda i: (i, 0, 0)),
                  smem_blk(lambda i: (jnp.minimum(i + 1, n_tiles - 1), 0, 0)),
                  pl.BlockSpec(memory_space=pl.ANY),
                  pl.BlockSpec((K, T, 1), lambda i: (0, i, 0)), row, row, vec, vec],
        out_specs=[row, row],
        out_shape=[jax.ShapeDtypeStruct((N, D), F32), jax.ShapeDtypeStruct((N, D), BF16)],
        scratch_shapes=[pltpu.VMEM((2, K, T, D // 2), jnp.uint32), pltpu.SemaphoreType.DMA((2,))],
        compiler_params=_params("arbitrary"),
        name="moe_combine_ln",
    )(pos3, pos3, y_sorted, gates.reshape(K, N, 1), h32, shared, g.reshape(1, D), b.reshape(1, D))


def _router_body(x_ref, w_ref, bias_ref, ek_ref, rk_ref, gk_ref, cnt_ref, carry_sc, tri_sc,
                 *, n_groups, topk_groups, top_k, scale):
    i = pl.program_id(0)
    E, T = w_ref.shape[0], x_ref.shape[0]
    G, Eg = n_groups, w_ref.shape[0] // n_groups
    neg = -jnp.inf

    @pl.when(i == 0)
    def _():
        carry_sc[...] = jnp.zeros(carry_sc.shape, F32)
        r = lax.broadcasted_iota(jnp.int32, (T, T), 0)
        c = lax.broadcasted_iota(jnp.int32, (T, T), 1)
        tri_sc[...] = jnp.where(r < c, 1.0, 0.0).astype(BF16)

    logits = lax.dot_general(w_ref[...], x_ref[...], (((1,), (1,)), ((), ())), preferred_element_type=F32)
    scores = jax.nn.sigmoid(logits)
    choice = scores + bias_ref[...]
    sub = lax.broadcasted_iota(jnp.int32, (Eg, T), 0)
    sc_g = [scores[g * Eg:(g + 1) * Eg] for g in range(G)]
    ch_g = [choice[g * Eg:(g + 1) * Eg] for g in range(G)]

    def first_index(hit, idx, size):
        return jnp.min(jnp.where(hit, idx, size), axis=0, keepdims=True)

    gs = []
    for x in ch_g:
        m1 = jnp.max(x, axis=0, keepdims=True)
        first = first_index(x == m1, sub, Eg)
        m2 = jnp.max(jnp.where(sub == first, neg, x), axis=0, keepdims=True)
        gs.append(m1 + m2)
    gs = jnp.concatenate(gs, axis=0)
    gid = lax.broadcasted_iota(jnp.int32, (G, T), 0)
    g_on = jnp.zeros((G, T), F32)
    for _ in range(topk_groups):
        m = jnp.max(gs, axis=0, keepdims=True)
        pick = gid == first_index(gs == m, gid, G)
        g_on = jnp.where(pick, 1.0, g_on)
        gs = jnp.where(pick, neg, gs)
    mk = [jnp.where(g_on[g:g + 1] > 0.0, ch_g[g], neg) for g in range(G)]

    eid = [sub + g * Eg for g in range(G)]
    sel = [jnp.zeros((Eg, T), F32) for _ in range(G)]
    e_k, s_k = [], []
    for _ in range(top_k):
        m = mk[0]
        for g in range(1, G):
            m = jnp.maximum(m, mk[g])
        m = jnp.max(m, axis=0, keepdims=True)
        cand = jnp.where(mk[0] == m, eid[0], E)
        for g in range(1, G):
            cand = jnp.minimum(cand, jnp.where(mk[g] == m, eid[g], E))
        first = jnp.min(cand, axis=0, keepdims=True)
        s = jnp.zeros((Eg, T), F32)
        for g in range(G):
            pick = eid[g] == first
            s = s + jnp.where(pick, sc_g[g], 0.0)
            mk[g] = jnp.where(pick, neg, mk[g])
            sel[g] = jnp.where(pick, 1.0, sel[g])
        e_k.append(first)
        s_k.append(jnp.sum(s, axis=0, keepdims=True))
    denom = s_k[0]
    for s in s_k[1:]:
        denom = denom + s
    gates = [s / denom * scale for s in s_k]

    sel_b = jnp.concatenate(sel, axis=0).astype(BF16)
    rank = (jnp.dot(sel_b, tri_sc[...], preferred_element_type=F32)
            + _lane_tile(carry_sc[...], T))
    total = carry_sc[...] + jnp.dot(sel_b, jnp.ones((T, LANES), BF16), preferred_element_type=F32)
    carry_sc[...] = total
    cnt_ref[...] = total
    r_k = []
    for first in e_k:
        r = jnp.zeros((Eg, T), F32)
        for g in range(G):
            r = r + jnp.where(eid[g] == first, rank[g * Eg:(g + 1) * Eg], 0.0)
        r_k.append(jnp.sum(r, axis=0, keepdims=True))
    ek_ref[...] = jnp.concatenate(e_k, axis=0)
    rk_ref[...] = jnp.concatenate(r_k, axis=0).astype(jnp.int32)
    gk_ref[...] = jnp.concatenate(gates, axis=0)


def _router(hb, w_router, bias):
    N, D = hb.shape
    E = w_router.shape[1]
    T = _tile(N, 512)
    assert E // N_GROUPS == 8 and T % LANES == 0
    kn = pl.BlockSpec((TOP_K, T), lambda i: (0, i))
    ek, rk, gk, cnt = pl.pallas_call(
        functools.partial(_router_body, n_groups=N_GROUPS, topk_groups=TOPK_GROUPS, top_k=TOP_K,
                          scale=ROUTED_SCALE),
        grid=(N // T,),
        in_specs=[pl.BlockSpec((T, D), lambda i: (i, 0)),
                  pl.BlockSpec((E, D), lambda i: (0, 0)),
                  pl.BlockSpec((E, 1), lambda i: (0, 0))],
        out_specs=[kn, kn, kn, pl.BlockSpec((E, LANES), lambda i: (0, 0))],
        out_shape=[jax.ShapeDtypeStruct((TOP_K, N), jnp.int32), jax.ShapeDtypeStruct((TOP_K, N), jnp.int32),
                   jax.ShapeDtypeStruct((TOP_K, N), F32), jax.ShapeDtypeStruct((E, LANES), F32)],
        scratch_shapes=[pltpu.VMEM((E, LANES), F32), pltpu.VMEM((T, T), BF16)],
        compiler_params=_params("arbitrary"),
        name="moe_router",
    )(hb, w_router.T.astype(BF16), bias.astype(F32).reshape(E, 1))
    return ek, rk, gk, cnt[:, 0].astype(jnp.int32)


def _row_layout(ek, rk, counts):
    K, N = ek.shape
    E = counts.shape[0]
    C = EXPERT_ROWS
    n_blocks = -(-(N * K) // C) + E
    padded = (counts + C - 1) // C * C
    pad_end = jnp.cumsum(padded)
    expert = jnp.arange(E, dtype=jnp.int32)[:, None, None]
    dest = jnp.sum(jnp.where(ek[None] == expert, (pad_end - padded)[:, None, None], 0), axis=0) + rk
    token = jnp.broadcast_to(jnp.arange(N, dtype=jnp.int32)[None, :], (K, N))
    row_tok = jnp.zeros((n_blocks * C,), jnp.int32).at[dest.reshape(-1)].set(
        token.reshape(-1), unique_indices=True)
    block_start = jnp.arange(n_blocks, dtype=jnp.int32) * C
    block_e = jnp.minimum(jnp.sum(pad_end[None, :] <= block_start[:, None], axis=1), E - 1).astype(jnp.int32)
    n_used = (pad_end[-1:] // C).astype(jnp.int32)
    return dest, row_tok, block_e, n_used


def _moe_ln(h32, hb, hp, layer, w_router, router_bias, w_gate_up, w_down, sh_w_gate_up, sh_w_down, g, b, alpha):
    F = sh_w_gate_up.shape[1] // 2
    ek, rk, gates, counts = _router(hb, w_router, router_bias)
    dest, row_tok, block_e, n_used = _row_layout(ek, rk, counts)
    y_sorted = _routed_experts(hp, row_tok, block_e, n_used, w_gate_up, w_down, layer)
    shared = _shared_expert(hb, sh_w_gate_up[:, :F].astype(BF16), sh_w_gate_up[:, F:].astype(BF16),
                            sh_w_down.astype(BF16))
    return _combine_ln(y_sorted, dest, gates, h32, shared, g, b, alpha)


def _diff_attention(hb, w_qkv, lam, subln, w_o, lambda_init, B, S):
    N, D = hb.shape
    d = lam.shape[1]
    H = D // (2 * d)
    tn = _tile(D, 1024)
    tm = _tile(S, 512)
    tabs = _rope_tables_full(S, d, d ** -0.5 * LOG2E)
    n_rope_tiles = 2 * D // tn
    nsb = S // tm
    qkv = _matmul(
        hb, w_qkv.astype(BF16), out_dtype=BF16, tm=tm, tn=tn,
        epilogue=functools.partial(_ep_qkv, n_rope_tiles=n_rope_tiles),
        extras=(tabs,),
        extra_specs=(pl.BlockSpec((None, 2, tm, d),
                                  lambda j, i: (jnp.minimum(j // (n_rope_tiles // 2), 1), 0, i % nsb, 0)),),
        name="diff_qkv")
    full = lambda shape: pl.BlockSpec(shape, lambda b, h, i: (0, 0))
    o = _attention(qkv, qkv, qkv, B=B, S=S, H=H, n_maps=2, dk=d, dv=2 * d,
                   q_col0=0, k_col0=H, v_col0=2 * H, k_width=2 * d,
                   finalize=functools.partial(_fin_diff, lambda_init=lambda_init),
                   fin_args=(lam, subln.reshape(1, 2 * d)),
                   fin_specs=(full((4, d)), full((1, 2 * d))), name="diff_attn")
    return _matmul(o, w_o.astype(BF16), out_dtype=F32, epilogue=_ep_plain, name="diff_out")


def _mla_shared_kv(hb, w_a, kv_norm, w_b, H, vdim, S):
    N, D = hb.shape
    R = kv_norm.shape[0]
    rope = w_a.shape[1] - R
    hw = w_b.shape[1] // H
    nope = hw - vdim
    assert nope == vdim == LANES and rope <= LANES // 2
    tm = _tile(S, 512)
    nsb = S // tm
    c_kv = _matmul(hb, w_a[:, :R].astype(BF16), out_dtype=BF16, tm=tm, tn=R,
                   epilogue=functools.partial(_ep_rmsnorm, eps=RMS_EPS),
                   extras=(kv_norm.reshape(1, R),),
                   extra_specs=(pl.BlockSpec((1, R), lambda j, i: (0, 0)),), name="mla_kv_a")
    w_pe = jnp.pad(w_a[:, R:], ((0, 0), (0, LANES - rope))).astype(BF16)
    tabs = _rope_tables_half(S, rope, 1.0)
    k_pe = _matmul(hb, w_pe, out_dtype=BF16, tm=tm, tn=LANES, epilogue=_ep_kpe, extras=(tabs,),
                   extra_specs=(pl.BlockSpec((None, 3, tm, LANES), lambda j, i: (1, 0, i % nsb, 0)),),
                   name="mla_k_pe")
    w_b3 = w_b.reshape(R, H, hw)
    w_b2 = jnp.concatenate([w_b3[:, :, :nope].reshape(R, H * nope),
                            w_b3[:, :, nope:].reshape(R, H * (hw - nope))], axis=1).astype(BF16)
    kv = _matmul(c_kv, w_b2, out_dtype=BF16, epilogue=_ep_plain, name="mla_kv_b")
    return kv, k_pe


def _mla_attention(hb, kv, k_pe, w_dq, q_norm, w_uq, w_o, H, B, S):
    N, D = hb.shape
    Qr = q_norm.shape[0]
    hq = w_uq.shape[1] // H
    nope = kv.shape[1] // (2 * H)
    rope = hq - nope
    tm = _tile(S, 512)
    nsb = S // tm
    c_q = _matmul(hb, w_dq.astype(BF16), out_dtype=BF16, tm=tm, tn=Qr,
                  epilogue=functools.partial(_ep_rmsnorm, eps=RMS_EPS),
                  extras=(q_norm.reshape(1, Qr),),
                  extra_specs=(pl.BlockSpec((1, Qr), lambda j, i: (0, 0)),), name="mla_dq")
    scale = float(hq) ** -0.5 * LOG2E
    w_q = jnp.pad(w_uq.reshape(Qr, H, hq), ((0, 0), (0, 0), (0, 2 * LANES - hq)))
    w_q = w_q.reshape(Qr, H * 2 * LANES).astype(BF16)
    tabs = _rope_tables_half(S, rope, scale)
    q = _matmul(c_q, w_q, out_dtype=BF16, tm=tm,
                epilogue=functools.partial(_ep_mla_q, scale=scale), extras=(tabs,),
                extra_specs=(pl.BlockSpec((None, 3, tm, LANES), lambda j, i: (0, 0, i % nsb, 0)),),
                name="mla_uq")
    o = _attention(q, kv, kv, B=B, S=S, H=H, n_maps=1, dk=2 * LANES, dv=nope,
                   q_col0=0, k_col0=0, v_col0=H, k_width=nope, kpe=k_pe,
                   finalize=_fin_plain, name="mla_attn")
    return _matmul(o, w_o.astype(BF16), out_dtype=F32, epilogue=_ep_plain, name="mla_out")


def kernel(x, ln_g, ln_b, a_w_qkv, a_lambda, a_subln, a_w_o, kv_w_a, kv_norm, kv_w_b, b_w_dq, b_q_norm, b_w_uq, b_w_o, moe_w_router, moe_router_bias, moe_w_gate_up, moe_w_down, moe_sh_w_gate_up, moe_sh_w_down):
    B, S, D = x.shape
    depth = ln_g.shape[0]
    n_a = a_w_qkv.shape[0]
    alpha = (2 * depth) ** 0.25
    rope_dim = kv_w_a.shape[1] - kv_norm.shape[0]
    mla_heads = (b_w_uq.shape[2] - kv_w_b.shape[1] + b_w_o.shape[1]) // rope_dim
    h32 = x.reshape(B * S, D)
    hb = h32.astype(BF16)
    kv = k_pe = None
    for l in range(depth):
        if l < n_a:
            lambda_init = 0.8 - 0.6 * math.exp(-0.3 * l)
            y = _diff_attention(hb, a_w_qkv[l], a_lambda[l], a_subln[l], a_w_o[l], lambda_init, B, S)
        else:
            if l == n_a:
                kv, k_pe = _mla_shared_kv(hb, kv_w_a, kv_norm, kv_w_b, mla_heads,
                                          b_w_o.shape[1] // mla_heads, S)
            j = l - n_a
            y = _mla_attention(hb, kv, k_pe, b_w_dq[j], b_q_norm[j], b_w_uq[j], b_w_o[j], mla_heads, B, S)
        h32, hb, hp = _ln_residual(h32, y, ln_g[l, 0], ln_b[l, 0], alpha)
        h32, hb = _moe_ln(h32, hb, hp, l, moe_w_router[l], moe_router_bias[l], moe_w_gate_up, moe_w_down,
                          moe_sh_w_gate_up[l], moe_sh_w_down[l], ln_g[l, 1], ln_b[l, 1], alpha)
    return h32.reshape(B, S, D)
```

```python
import functools
import math

import jax
import jax.numpy as jnp
from jax import lax
from jax.experimental import pallas as pl
from jax.experimental.pallas import tpu as pltpu

F32 = jnp.float32
BF16 = jnp.bfloat16

N_GROUPS = 8
TOPK_GROUPS = 4
TOP_K = 8
ROUTED_SCALE = 2.5
ROPE_THETA = 10000.0
LN_EPS = 1e-5
RMS_EPS = 1e-6
LOG2E = math.log2(math.e)

LANES = 128
V7X_VMEM_LIMIT = 56 * 1024 * 1024
EXPERT_ROWS = 512
ROW_ISSUE_STRIDE = 67
COMBINE_TOKENS = 64
ATTN_QUERY_TILE = 1024
ATTN_KEY_TILE = 512


def _params(*sem):
    return pltpu.CompilerParams(dimension_semantics=sem, vmem_limit_bytes=V7X_VMEM_LIMIT)


def _tile(n, pref):
    if n <= pref:
        return n
    t = pref - pref % LANES
    while t >= LANES:
        if n % t == 0:
            return t
        t -= LANES
    return n


def _mm_body(x_ref, w_ref, *rest, epilogue, n_extra):
    extra, o_ref = rest[:n_extra], rest[n_extra]
    acc = jnp.dot(x_ref[...], w_ref[...], preferred_element_type=F32)
    epilogue(acc, o_ref, *extra)


def _matmul(x, w, *, out_dtype, epilogue, extras=(), extra_specs=(), tm=512, tn=1024, name):
    M, K = x.shape
    N = w.shape[1]
    tm, tn = _tile(M, tm), _tile(N, tn)
    return pl.pallas_call(
        functools.partial(_mm_body, epilogue=epilogue, n_extra=len(extras)),
        grid=(N // tn, M // tm),
        in_specs=[pl.BlockSpec((tm, K), lambda j, i: (i, 0)),
                  pl.BlockSpec((K, tn), lambda j, i: (0, j)),
                  *extra_specs],
        out_specs=pl.BlockSpec((tm, tn), lambda j, i: (i, j)),
        out_shape=jax.ShapeDtypeStruct((M, N), out_dtype),
        compiler_params=_params("parallel", "parallel"),
        name=name,
    )(x, w, *extras)


def _ep_plain(acc, o_ref):
    o_ref[...] = acc.astype(o_ref.dtype)


def _ep_rmsnorm(acc, o_ref, g_ref, *, eps):
    ms = jnp.mean(acc * acc, axis=-1, keepdims=True)
    o_ref[...] = (acc * lax.rsqrt(ms + eps) * g_ref[...]).astype(o_ref.dtype)


def _rope_full_group(seg, cos, sin):
    return seg * cos + pltpu.roll(seg, LANES // 2, 1) * sin


def _rope_half_group(seg, c, sa, sb):
    return seg * c + pltpu.roll(seg, 3 * LANES // 4, 1) * sa + pltpu.roll(seg, LANES // 4, 1) * sb


def _ep_qkv(acc, o_ref, tab_ref, *, n_rope_tiles):
    j = pl.program_id(0)

    @pl.when(j < n_rope_tiles)
    def _():
        cos, sin = tab_ref[0], tab_ref[1]
        for g in range(acc.shape[1] // LANES):
            sl = slice(g * LANES, (g + 1) * LANES)
            o_ref[:, sl] = _rope_full_group(acc[:, sl], cos, sin).astype(o_ref.dtype)

    @pl.when(j >= n_rope_tiles)
    def _():
        o_ref[...] = acc.astype(o_ref.dtype)


def _ep_mla_q(acc, o_ref, tab_ref, *, scale):
    c, sa, sb = tab_ref[0], tab_ref[1], tab_ref[2]
    for g in range(acc.shape[1] // LANES):
        sl = slice(g * LANES, (g + 1) * LANES)
        seg = acc[:, sl]
        out = seg * scale if g % 2 == 0 else _rope_half_group(seg, c, sa, sb)
        o_ref[:, sl] = out.astype(o_ref.dtype)


def _ep_kpe(acc, o_ref, tab_ref):
    o_ref[...] = _rope_half_group(acc, tab_ref[0], tab_ref[1], tab_ref[2]).astype(o_ref.dtype)


def _rope_tables_full(S, d, scale):
    inv = ROPE_THETA ** (-jnp.arange(0, d, 2, dtype=F32) / d)
    ang = jnp.arange(S, dtype=F32)[:, None] * inv[None, :]
    cos = jnp.concatenate([jnp.cos(ang), jnp.cos(ang)], axis=-1)
    sin = jnp.concatenate([-jnp.sin(ang), jnp.sin(ang)], axis=-1)
    t = jnp.stack([cos, sin])
    return jnp.stack([t * scale, t])


def _rope_tables_half(S, d, scale):
    inv = ROPE_THETA ** (-jnp.arange(0, d, 2, dtype=F32) / d)
    ang = jnp.arange(S, dtype=F32)[:, None] * inv[None, :]
    cos, sin = jnp.cos(ang), jnp.sin(ang)
    z = jnp.zeros_like(cos)
    pad = jnp.zeros((S, LANES - d), F32)
    c = jnp.concatenate([cos, cos, pad], axis=-1)
    sa = jnp.concatenate([-sin, z, pad], axis=-1)
    sb = jnp.concatenate([z, sin, pad], axis=-1)
    t = jnp.stack([c, sa, sb])
    return jnp.stack([t * scale, t])


def _pack_halves(x):
    half = x.shape[1] // 2
    lo = lax.bitcast_convert_type(x[:, :half].astype(BF16).astype(F32), jnp.uint32)
    hi = lax.bitcast_convert_type(x[:, half:].astype(BF16).astype(F32), jnp.uint32)
    return (lo >> 16) | (hi & jnp.uint32(0xFFFF0000))


def _unpack_halves(w):
    lo = lax.bitcast_convert_type(w << 16, F32)
    hi = lax.bitcast_convert_type(w & jnp.uint32(0xFFFF0000), F32)
    return lo, hi


def _ln_body(h_ref, y_ref, g_ref, b_ref, of_ref, ob_ref, op_ref, *, alpha):
    z = alpha * h_ref[...] + y_ref[...]
    mu = jnp.mean(z, axis=-1, keepdims=True)
    zc = z - mu
    var = jnp.mean(zc * zc, axis=-1, keepdims=True)
    o = zc * lax.rsqrt(var + LN_EPS) * g_ref[...] + b_ref[...]
    of_ref[...] = o
    ob_ref[...] = o.astype(BF16)
    op_ref[...] = _pack_halves(o)


def _ln_residual(h, y, g, b, alpha):
    N, D = h.shape
    tm = _tile(N, 256)
    row = pl.BlockSpec((tm, D), lambda i: (i, 0))
    half = pl.BlockSpec((tm, D // 2), lambda i: (i, 0))
    vec = pl.BlockSpec((1, D), lambda i: (0, 0))
    return pl.pallas_call(
        functools.partial(_ln_body, alpha=alpha),
        grid=(N // tm,),
        in_specs=[row, row, vec, vec],
        out_specs=[row, row, half],
        out_shape=[jax.ShapeDtypeStruct((N, D), F32), jax.ShapeDtypeStruct((N, D), BF16),
                   jax.ShapeDtypeStruct((N, D // 2), jnp.uint32)],
        compiler_params=_params("parallel"),
        name="ln_residual",
    )(h, y, g.reshape(1, D), b.reshape(1, D))


def _attn_body(*refs, n_maps, dk, tq, tk, has_kpe, finalize):
    q_ref, k_ref, v_ref = refs[:3]
    pos = 3
    kpe_ref = None
    if has_kpe:
        kpe_ref, pos = refs[3], 4
    fin_refs = refs[pos:-5]
    o_ref, m_sc, l_sc, acc_sc, s_sc = refs[-5:]
    qi = pl.program_id(2)
    ratio = tq // tk

    m_sc[...] = jnp.full(m_sc.shape, -jnp.inf, F32)
    l_sc[...] = jnp.zeros(l_sc.shape, F32)
    acc_sc[...] = jnp.zeros(acc_sc.shape, F32)

    def scores(ki, m, r0):
        off = pl.multiple_of(ki * tk, tk)
        k = k_ref[pl.ds(off, tk), :]
        if has_kpe:
            k = jnp.concatenate([k, kpe_ref[pl.ds(off, tk), :]], axis=-1)
        return lax.dot_general(q_ref[r0:, m * dk:(m + 1) * dk], k[:, m * dk:(m + 1) * dk],
                               (((1,), (1,)), ((), ())), preferred_element_type=F32)

    def accumulate(ki, m, s, r0):
        v = v_ref[pl.ds(pl.multiple_of(ki * tk, tk), tk), :]
        m_old = m_sc[m, r0:]
        m_new = jnp.maximum(m_old, jnp.max(s, axis=-1, keepdims=True))
        p = jnp.exp2(s - _lane_tile(m_new, tk))
        alpha = jnp.exp2(m_old - m_new)
        l_sc[m, r0:] = alpha * l_sc[m, r0:] + jnp.sum(p, axis=-1, keepdims=True)
        acc_sc[m, r0:] = (_lane_tile(alpha, acc_sc.shape[2]) * acc_sc[m, r0:]
                          + jnp.dot(p.astype(BF16), v, preferred_element_type=F32))
        m_sc[m, r0:] = m_new

    def causal(s):
        row = lax.broadcasted_iota(jnp.int32, s.shape, 0)
        col = lax.broadcasted_iota(jnp.int32, s.shape, 1)
        return jnp.where(col <= row, s, -jnp.inf)

    for m in range(n_maps):
        s_sc[m] = scores(0, m, 0)

    def full_step(ki, carry):
        for m in range(n_maps):
            s_next = scores(ki + 1, m, 0)
            accumulate(ki, m, s_sc[m], 0)
            s_sc[m] = s_next
        return carry

    n_full = ratio * qi
    lax.fori_loop(0, n_full, full_step, 0)
    for m in range(n_maps):
        accumulate(n_full, m, causal(s_sc[m]), 0)
    for j in range(1, ratio):
        for m in range(n_maps):
            accumulate(n_full + j, m, causal(scores(n_full + j, m, j * tk)), j * tk)
    finalize(o_ref, l_sc, acc_sc, *fin_refs)


def _lane_tile(x, width):
    return jnp.tile(x, (1, width // LANES))


def _fin_plain(o_ref, l_sc, acc_sc):
    o_ref[...] = (acc_sc[0] / _lane_tile(l_sc[0], acc_sc.shape[2])).astype(o_ref.dtype)


def _fin_diff(o_ref, l_sc, acc_sc, lam_ref, subln_ref, *, lambda_init):
    lam = lam_ref[...]
    lam_full = (jnp.exp(jnp.sum(lam[0:1] * lam[1:2], axis=-1, keepdims=True))
                - jnp.exp(jnp.sum(lam[2:3] * lam[3:4], axis=-1, keepdims=True)) + lambda_init)
    dv = acc_sc.shape[2]
    o = acc_sc[0] / _lane_tile(l_sc[0], dv) - lam_full * (acc_sc[1] / _lane_tile(l_sc[1], dv))
    ms = jnp.mean(o * o, axis=-1, keepdims=True)
    o = o * lax.rsqrt(ms + LN_EPS) * subln_ref[...] * (1.0 - lambda_init)
    o_ref[...] = o.astype(o_ref.dtype)


def _attention(q, k, v, *, B, S, H, n_maps, dk, dv, q_col0, k_col0, v_col0, k_width,
               kpe=None, finalize, fin_args=(), fin_specs=(), name):
    tk = _tile(S, ATTN_KEY_TILE)
    T = _tile(S, ATTN_QUERY_TILE)
    assert T % tk == 0
    nq = S // T
    in_specs = [pl.BlockSpec((T, n_maps * dk), lambda b, h, i: (b * nq + i, q_col0 + h)),
                pl.BlockSpec((S, k_width), lambda b, h, i: (b, k_col0 + h)),
                pl.BlockSpec((S, dv), lambda b, h, i: (b, v_col0 + h))]
    args = [q, k, v]
    if kpe is not None:
        in_specs.append(pl.BlockSpec((S, kpe.shape[1]), lambda b, h, i: (b, 0)))
        args.append(kpe)
    return pl.pallas_call(
        functools.partial(_attn_body, n_maps=n_maps, dk=dk, tq=T, tk=tk, has_kpe=kpe is not None,
                          finalize=finalize),
        grid=(B, H, nq),
        in_specs=in_specs + list(fin_specs),
        out_specs=pl.BlockSpec((T, dv), lambda b, h, i: (b * nq + i, h)),
        out_shape=jax.ShapeDtypeStruct((B * S, H * dv), BF16),
        scratch_shapes=[pltpu.VMEM((n_maps, T, LANES), F32), pltpu.VMEM((n_maps, T, LANES), F32),
                        pltpu.VMEM((n_maps, T, dv), F32), pltpu.VMEM((n_maps, T, tk), F32)],
        compiler_params=_params("parallel", "parallel", "parallel"),
        name=name,
    )(*args, *fin_args)


def _shared_body(x_ref, wg_ref, wu_ref, wd_ref, o_ref):
    x = x_ref[...]
    g = jnp.dot(x, wg_ref[...], preferred_element_type=F32)
    u = jnp.dot(x, wu_ref[...], preferred_element_type=F32)
    a = (g * jax.nn.sigmoid(g) * u).astype(BF16)
    o_ref[...] = jnp.dot(a, wd_ref[...], preferred_element_type=F32)


def _shared_expert(xb, wg, wu, wd):
    N, D = xb.shape
    F = wg.shape[1]
    tm = _tile(N, 512)
    full = lambda shape: pl.BlockSpec(shape, lambda i: (0, 0))
    return pl.pallas_call(
        _shared_body,
        grid=(N // tm,),
        in_specs=[pl.BlockSpec((tm, D), lambda i: (i, 0)), full((D, F)), full((D, F)), full((F, D))],
        out_specs=pl.BlockSpec((tm, D), lambda i: (i, 0)),
        out_shape=jax.ShapeDtypeStruct((N, D), F32),
        compiler_params=_params("parallel"),
        name="moe_shared",
    )(xb, wg, wu, wd)


def _expert_body(be_ref, nu_ref, tok_ref, tokn_ref, x_hbm, wgu_ref, wd_ref, y_ref,
                 xbuf, sem, wg_sc, wu_sc, wd_sc, *, rows, ffn, k_chunk, n_chunk):
    b = pl.program_id(0)
    last = pl.num_programs(0) - 1
    nu = nu_ref[0]
    slot = b % 2

    def row_copy(idx_ref, r, s):
        return pltpu.make_async_copy(x_hbm.at[pl.ds(idx_ref[0, 0, r], 1), :],
                                     xbuf.at[s, pl.ds(r, 1), :], sem.at[s])

    def wait_gather(s):
        pltpu.make_async_copy(x_hbm.at[pl.ds(0, rows), :], xbuf.at[s], sem.at[s]).wait()

    @pl.when(b == 0)
    def _():
        def body(r, carry):
            row_copy(tok_ref, r, 0).start()
            return carry
        lax.fori_loop(0, rows, body, 0, unroll=8)

    @pl.when(b <= nu)
    def _():
        wait_gather(slot)

    @pl.when(b < nu)
    def _():
        @pl.when((b == 0) | (be_ref[b] != be_ref[jnp.maximum(b - 1, 0)]))
        def _():
            wg_sc[...] = wgu_ref[:, :ffn].astype(BF16)
            wu_sc[...] = wgu_ref[:, ffn:].astype(BF16)
            wd_sc[...] = wd_ref[...].astype(BF16)

        half = xbuf.shape[2]
        n_chunks = half // k_chunk
        n_down = half // n_chunk
        work = [2 * k_chunk * 2 * ffn] * n_chunks + [2 * n_chunk * ffn] * n_down
        edges = [rows * sum(work[:i]) // sum(work) for i in range(len(work) + 1)]

        def start_next_rows(phase):
            for i in range(edges[phase], edges[phase + 1]):
                row_copy(tokn_ref, (i * ROW_ISSUE_STRIDE) % rows, 1 - slot).start(priority=1)

        g = u = None
        for c in range(n_chunks):
            start_next_rows(c)
            lo, hi = _unpack_halves(xbuf[slot, :, c * k_chunk:(c + 1) * k_chunk])
            for part, k0 in ((lo, c * k_chunk), (hi, half + c * k_chunk)):
                xc = part.astype(BF16)
                dg = jnp.dot(xc, wg_sc[k0:k0 + k_chunk, :], preferred_element_type=F32)
                du = jnp.dot(xc, wu_sc[k0:k0 + k_chunk, :], preferred_element_type=F32)
                g = dg if g is None else g + dg
                u = du if u is None else u + du
        a = (g * jax.nn.sigmoid(g) * u).astype(BF16)
        for c in range(n_down):
            start_next_rows(n_chunks + c)
            n0 = c * n_chunk
            y_lo = jnp.dot(a, wd_sc[:, n0:n0 + n_chunk], preferred_element_type=F32)
            y_hi = jnp.dot(a, wd_sc[:, half + n0:half + n0 + n_chunk], preferred_element_type=F32)
            y_ref[:, n0:n0 + n_chunk] = _pack_halves(jnp.concatenate([y_lo, y_hi], axis=1))

    @pl.when(b >= nu)
    def _():
        y_ref[...] = jnp.zeros(y_ref.shape, y_ref.dtype)

    @pl.when((b == last) & (b < nu))
    def _():
        wait_gather(1 - slot)


def _routed_experts(hp, row_tok, block_e, n_used, w_gate_up, w_down, layer):
    N, half = hp.shape
    D = 2 * half
    _, E, _, F2 = w_gate_up.shape
    F = F2 // 2
    C = EXPERT_ROWS
    n_blocks = block_e.shape[0]
    k_chunk = _tile(half, 256)
    n_chunk = _tile(half, 512)
    assert math.gcd(ROW_ISSUE_STRIDE, C) == 1
    tok3 = row_tok.reshape(n_blocks, 1, C)
    smem_blk = lambda fn: pl.BlockSpec((1, 1, C), fn, memory_space=pltpu.SMEM)
    grid_spec = pltpu.PrefetchScalarGridSpec(
        num_scalar_prefetch=2,
        grid=(n_blocks,),
        in_specs=[smem_blk(lambda b, be, nu: (b, 0, 0)),
                  smem_blk(lambda b, be, nu: (jnp.minimum(b + 1, n_blocks - 1), 0, 0)),
                  pl.BlockSpec(memory_space=pl.ANY),
                  pl.BlockSpec((None, None, D, F2), lambda b, be, nu: (layer, be[b], 0, 0)),
                  pl.BlockSpec((None, None, F, D), lambda b, be, nu: (layer, be[b], 0, 0))],
        out_specs=pl.BlockSpec((C, half), lambda b, be, nu: (b, 0)),
        scratch_shapes=[pltpu.VMEM((2, C, half), jnp.uint32), pltpu.SemaphoreType.DMA((2,)),
                        pltpu.VMEM((D, F), BF16), pltpu.VMEM((D, F), BF16), pltpu.VMEM((F, D), BF16)],
    )
    return pl.pallas_call(
        functools.partial(_expert_body, rows=C, ffn=F, k_chunk=k_chunk, n_chunk=n_chunk),
        grid_spec=grid_spec,
        out_shape=jax.ShapeDtypeStruct((n_blocks * C, half), jnp.uint32),
        compiler_params=_params("arbitrary"),
        name="moe_experts",
    )(block_e, n_used, tok3, tok3, hp, w_gate_up, w_down)


def _combine_body(pos_ref, posn_ref, y_hbm, gate_ref, h_ref, sh_ref, g_ref, b_ref, of_ref, ob_ref,
                  ybuf, sem, *, tokens, top_k, alpha):
    i = pl.program_id(0)
    last = pl.num_programs(0) - 1
    slot = i % 2

    def row_copy(idx_ref, t, k, s):
        return pltpu.make_async_copy(y_hbm.at[pl.ds(idx_ref[0, 0, t * top_k + k], 1), :],
                                     ybuf.at[s, k, pl.ds(t, 1), :], sem.at[s])

    def wait_gather(s):
        for k in range(top_k):
            pltpu.make_async_copy(y_hbm.at[pl.ds(0, tokens), :], ybuf.at[s, k], sem.at[s]).wait()

    @pl.when(i == 0)
    def _():
        def body(t, carry):
            for k in range(top_k):
                row_copy(pos_ref, t, k, 0).start()
            return carry
        lax.fori_loop(0, tokens, body, 0)

    wait_gather(slot)

    r_lo = r_hi = None
    for k in range(top_k):
        for t in range(tokens):
            row_copy(posn_ref, t, k, 1 - slot).start(priority=t % 2)
        lo, hi = _unpack_halves(ybuf[slot, k])
        gate = gate_ref[k]
        r_lo = gate * lo if r_lo is None else r_lo + gate * lo
        r_hi = gate * hi if r_hi is None else r_hi + gate * hi

    @pl.when(i == last)
    def _():
        wait_gather(1 - slot)

    routed = jnp.concatenate([r_lo, r_hi], axis=1)
    z = alpha * h_ref[...] + (routed + sh_ref[...])
    mu = jnp.mean(z, axis=-1, keepdims=True)
    zc = z - mu
    var = jnp.mean(zc * zc, axis=-1, keepdims=True)
    o = zc * lax.rsqrt(var + LN_EPS) * g_ref[...] + b_ref[...]
    of_ref[...] = o
    ob_ref[...] = o.astype(BF16)


def _combine_ln(y_sorted, dest, gates, h32, shared, g, b, alpha):
    N, D = h32.shape
    K = dest.shape[0]
    T = COMBINE_TOKENS
    assert N % T == 0
    n_tiles = N // T
    pos3 = dest.T.reshape(n_tiles, 1, T * K)
    smem_blk = lambda fn: pl.BlockSpec((1, 1, T * K), fn, memory_space=pltpu.SMEM)
    row = pl.BlockSpec((T, D), lambda i: (i, 0))
    vec = pl.BlockSpec((1, D), lambda i: (0, 0))
    return pl.pallas_call(
        functools.partial(_combine_body, tokens=T, top_k=K, alpha=alpha),
        grid=(n_tiles,),
        in_specs=[smem_blk(lambda i: (i, 0, 0)),
                  smem_blk(lambda i: (jnp.minimum(i + 1, n_tiles - 1), 0, 0)),
                  pl.BlockSpec(memory_space=pl.ANY),
                  pl.BlockSpec((K, T, 1), lambda i: (0, i, 0)), row, row, vec, vec],
        out_specs=[row, row],
        out_shape=[jax.ShapeDtypeStruct((N, D), F32), jax.ShapeDtypeStruct((N, D), BF16)],
        scratch_shapes=[pltpu.VMEM((2, K, T, D // 2), jnp.uint32), pltpu.SemaphoreType.DMA((2,))],
        compiler_params=_params("arbitrary"),
        name="moe_combine_ln",
    )(pos3, pos3, y_sorted, gates.reshape(K, N, 1), h32, shared, g.reshape(1, D), b.reshape(1, D))


def _router_body(x_ref, w_ref, bias_ref, ek_ref, rk_ref, gk_ref, cnt_ref, carry_sc, tri_sc,
                 *, n_groups, topk_groups, top_k, scale):
    i = pl.program_id(0)
    E, T = w_ref.shape[0], x_ref.shape[0]
    G, Eg = n_groups, w_ref.shape[0] // n_groups
    neg = -jnp.inf

    @pl.when(i == 0)
    def _():
        carry_sc[...] = jnp.zeros(carry_sc.shape, F32)
        r = lax.broadcasted_iota(jnp.int32, (T, T), 0)
        c = lax.broadcasted_iota(jnp.int32, (T, T), 1)
        tri_sc[...] = jnp.where(r < c, 1.0, 0.0).astype(BF16)

    logits = lax.dot_general(w_ref[...], x_ref[...], (((1,), (1,)), ((), ())), preferred_element_type=F32)
    scores = jax.nn.sigmoid(logits)
    choice = scores + bias_ref[...]
    sub = lax.broadcasted_iota(jnp.int32, (Eg, T), 0)
    sc_g = [scores[g * Eg:(g + 1) * Eg] for g in range(G)]
    ch_g = [choice[g * Eg:(g + 1) * Eg] for g in range(G)]

    def first_index(hit, idx, size):
        return jnp.min(jnp.where(hit, idx, size), axis=0, keepdims=True)

    gs = []
    for x in ch_g:
        m1 = jnp.max(x, axis=0, keepdims=True)
        first = first_index(x == m1, sub, Eg)
        m2 = jnp.max(jnp.where(sub == first, neg, x), axis=0, keepdims=True)
        gs.append(m1 + m2)
    gs = jnp.concatenate(gs, axis=0)
    gid = lax.broadcasted_iota(jnp.int32, (G, T), 0)
    g_on = jnp.zeros((G, T), F32)
    for _ in range(topk_groups):
        m = jnp.max(gs, axis=0, keepdims=True)
        pick = gid == first_index(gs == m, gid, G)
        g_on = jnp.where(pick, 1.0, g_on)
        gs = jnp.where(pick, neg, gs)
    mk = [jnp.where(g_on[g:g + 1] > 0.0, ch_g[g], neg) for g in range(G)]

    eid = [sub + g * Eg for g in range(G)]
    sel = [jnp.zeros((Eg, T), F32) for _ in range(G)]
    e_k, s_k = [], []
    for _ in range(top_k):
        m = mk[0]
        for g in range(1, G):
            m = jnp.maximum(m, mk[g])
        m = jnp.max(m, axis=0, keepdims=True)
        cand = jnp.where(mk[0] == m, eid[0], E)
        for g in range(1, G):
            cand = jnp.minimum(cand, jnp.where(mk[g] == m, eid[g], E))
        first = jnp.min(cand, axis=0, keepdims=True)
        s = jnp.zeros((Eg, T), F32)
        for g in range(G):
            pick = eid[g] == first
            s = s + jnp.where(pick, sc_g[g], 0.0)
            mk[g] = jnp.where(pick, neg, mk[g])
            sel[g] = jnp.where(pick, 1.0, sel[g])
        e_k.append(first)
        s_k.append(jnp.sum(s, axis=0, keepdims=True))
    denom = s_k[0]
    for s in s_k[1:]:
        denom = denom + s
    gates = [s / denom * scale for s in s_k]

    sel_b = jnp.concatenate(sel, axis=0).astype(BF16)
    rank = (jnp.dot(sel_b, tri_sc[...], preferred_element_type=F32)
            + _lane_tile(carry_sc[...], T))
    total = carry_sc[...] + jnp.dot(sel_b, jnp.ones((T, LANES), BF16), preferred_element_type=F32)
    carry_sc[...] = total
    cnt_ref[...] = total
    r_k = []
    for first in e_k:
        r = jnp.zeros((Eg, T), F32)
        for g in range(G):
            r = r + jnp.where(eid[g] == first, rank[g * Eg:(g + 1) * Eg], 0.0)
        r_k.append(jnp.sum(r, axis=0, keepdims=True))
    ek_ref[...] = jnp.concatenate(e_k, axis=0)
    rk_ref[...] = jnp.concatenate(r_k, axis=0).astype(jnp.int32)
    gk_ref[...] = jnp.concatenate(gates, axis=0)


def _router(hb, w_router, bias):
    N, D = hb.shape
    E = w_router.shape[1]
    T = _tile(N, 512)
    assert E // N_GROUPS == 8 and T % LANES == 0
    kn = pl.BlockSpec((TOP_K, T), lambda i: (0, i))
    ek, rk, gk, cnt = pl.pallas_call(
        functools.partial(_router_body, n_groups=N_GROUPS, topk_groups=TOPK_GROUPS, top_k=TOP_K,
                          scale=ROUTED_SCALE),
        grid=(N // T,),
        in_specs=[pl.BlockSpec((T, D), lambda i: (i, 0)),
                  pl.BlockSpec((E, D), lambda i: (0, 0)),
                  pl.BlockSpec((E, 1), lambda i: (0, 0))],
        out_specs=[kn, kn, kn, pl.BlockSpec((E, LANES), lambda i: (0, 0))],
        out_shape=[jax.ShapeDtypeStruct((TOP_K, N), jnp.int32), jax.ShapeDtypeStruct((TOP_K, N), jnp.int32),
                   jax.ShapeDtypeStruct((TOP_K, N), F32), jax.ShapeDtypeStruct((E, LANES), F32)],
        scratch_shapes=[pltpu.VMEM((E, LANES), F32), pltpu.VMEM((T, T), BF16)],
        compiler_params=_params("arbitrary"),
        name="moe_router",
    )(hb, w_router.T.astype(BF16), bias.astype(F32).reshape(E, 1))
    return ek, rk, gk, cnt[:, 0].astype(jnp.int32)


def _row_layout(ek, rk, counts):
    K, N = ek.shape
    E = counts.shape[0]
    C = EXPERT_ROWS
    n_blocks = -(-(N * K) // C) + E
    padded = (counts + C - 1) // C * C
    pad_end = jnp.cumsum(padded)
    expert = jnp.arange(E, dtype=jnp.int32)[:, None, None]
    dest = jnp.sum(jnp.where(ek[None] == expert, (pad_end - padded)[:, None, None], 0), axis=0) + rk
    token = jnp.broadcast_to(jnp.arange(N, dtype=jnp.int32)[None, :], (K, N))
    row_tok = jnp.zeros((n_blocks * C,), jnp.int32).at[dest.reshape(-1)].set(
        token.reshape(-1), unique_indices=True)
    block_start = jnp.arange(n_blocks, dtype=jnp.int32) * C
    block_e = jnp.minimum(jnp.sum(pad_end[None, :] <= block_start[:, None], axis=1), E - 1).astype(jnp.int32)
    n_used = (pad_end[-1:] // C).astype(jnp.int32)
    return dest, row_tok, block_e, n_used


def _moe_ln(h32, hb, hp, layer, w_router, router_bias, w_gate_up, w_down, sh_w_gate_up, sh_w_down, g, b, alpha):
    F = sh_w_gate_up.shape[1] // 2
    ek, rk, gates, counts = _router(hb, w_router, router_bias)
    dest, row_tok, block_e, n_used = _row_layout(ek, rk, counts)
    y_sorted = _routed_experts(hp, row_tok, block_e, n_used, w_gate_up, w_down, layer)
    shared = _shared_expert(hb, sh_w_gate_up[:, :F].astype(BF16), sh_w_gate_up[:, F:].astype(BF16),
                            sh_w_down.astype(BF16))
    return _combine_ln(y_sorted, dest, gates, h32, shared, g, b, alpha)


def _diff_attention(hb, w_qkv, lam, subln, w_o, lambda_init, B, S):
    N, D = hb.shape
    d = lam.shape[1]
    H = D // (2 * d)
    tn = _tile(D, 1024)
    tm = _tile(S, 512)
    tabs = _rope_tables_full(S, d, d ** -0.5 * LOG2E)
    n_rope_tiles = 2 * D // tn
    nsb = S // tm
    qkv = _matmul(
        hb, w_qkv.astype(BF16), out_dtype=BF16, tm=tm, tn=tn,
        epilogue=functools.partial(_ep_qkv, n_rope_tiles=n_rope_tiles),
        extras=(tabs,),
        extra_specs=(pl.BlockSpec((None, 2, tm, d),
                                  lambda j, i: (jnp.minimum(j // (n_rope_tiles // 2), 1), 0, i % nsb, 0)),),
        name="diff_qkv")
    full = lambda shape: pl.BlockSpec(shape, lambda b, h, i: (0, 0))
    o = _attention(qkv, qkv, qkv, B=B, S=S, H=H, n_maps=2, dk=d, dv=2 * d,
                   q_col0=0, k_col0=H, v_col0=2 * H, k_width=2 * d,
                   finalize=functools.partial(_fin_diff, lambda_init=lambda_init),
                   fin_args=(lam, subln.reshape(1, 2 * d)),
                   fin_specs=(full((4, d)), full((1, 2 * d))), name="diff_attn")
    return _matmul(o, w_o.astype(BF16), out_dtype=F32, epilogue=_ep_plain, name="diff_out")


def _mla_shared_kv(hb, w_a, kv_norm, w_b, H, vdim, S):
    N, D = hb.shape
    R = kv_norm.shape[0]
    rope = w_a.shape[1] - R
    hw = w_b.shape[1] // H
    nope = hw - vdim
    assert nope == vdim == LANES and rope <= LANES // 2
    tm = _tile(S, 512)
    nsb = S // tm
    c_kv = _matmul(hb, w_a[:, :R].astype(BF16), out_dtype=BF16, tm=tm, tn=R,
                   epilogue=functools.partial(_ep_rmsnorm, eps=RMS_EPS),
                   extras=(kv_norm.reshape(1, R),),
                   extra_specs=(pl.BlockSpec((1, R), lambda j, i: (0, 0)),), name="mla_kv_a")
    w_pe = jnp.pad(w_a[:, R:], ((0, 0), (0, LANES - rope))).astype(BF16)
    tabs = _rope_tables_half(S, rope, 1.0)
    k_pe = _matmul(hb, w_pe, out_dtype=BF16, tm=tm, tn=LANES, epilogue=_ep_kpe, extras=(tabs,),
                   extra_specs=(pl.BlockSpec((None, 3, tm, LANES), lambda j, i: (1, 0, i % nsb, 0)),),
                   name="mla_k_pe")
    w_b3 = w_b.reshape(R, H, hw)
    w_b2 = jnp.concatenate([w_b3[:, :, :nope].reshape(R, H * nope),
                            w_b3[:, :, nope:].reshape(R, H * (hw - nope))], axis=1).astype(BF16)
    kv = _matmul(c_kv, w_b2, out_dtype=BF16, epilogue=_ep_plain, name="mla_kv_b")
    return kv, k_pe


def _mla_attention(hb, kv, k_pe, w_dq, q_norm, w_uq, w_o, H, B, S):
    N, D = hb.shape
    Qr = q_norm.shape[0]
    hq = w_uq.shape[1] // H
    nope = kv.shape[1] // (2 * H)
    rope = hq - nope
    tm = _tile(S, 512)
    nsb = S // tm
    c_q = _matmul(hb, w_dq.astype(BF16), out_dtype=BF16, tm=tm, tn=Qr,
                  epilogue=functools.partial(_ep_rmsnorm, eps=RMS_EPS),
                  extras=(q_norm.reshape(1, Qr),),
                  extra_specs=(pl.BlockSpec((1, Qr), lambda j, i: (0, 0)),), name="mla_dq")
    scale = float(hq) ** -0.5 * LOG2E
    w_q = jnp.pad(w_uq.reshape(Qr, H, hq), ((0, 0), (0, 0), (0, 2 * LANES - hq)))
    w_q = w_q.reshape(Qr, H * 2 * LANES).astype(BF16)
    tabs = _rope_tables_half(S, rope, scale)
    q = _matmul(c_q, w_q, out_dtype=BF16, tm=tm,
                epilogue=functools.partial(_ep_mla_q, scale=scale), extras=(tabs,),
                extra_specs=(pl.BlockSpec((None, 3, tm, LANES), lambda j, i: (0, 0, i % nsb, 0)),),
                name="mla_uq")
    o = _attention(q, kv, kv, B=B, S=S, H=H, n_maps=1, dk=2 * LANES, dv=nope,
                   q_col0=0, k_col0=0, v_col0=H, k_width=nope, kpe=k_pe,
                   finalize=_fin_plain, name="mla_attn")
    return _matmul(o, w_o.astype(BF16), out_dtype=F32, epilogue=_ep_plain, name="mla_out")


def kernel(x, ln_g, ln_b, a_w_qkv, a_lambda, a_subln, a_w_o, kv_w_a, kv_norm, kv_w_b, b_w_dq, b_q_norm, b_w_uq, b_w_o, moe_w_router, moe_router_bias, moe_w_gate_up, moe_w_down, moe_sh_w_gate_up, moe_sh_w_down):
    B, S, D = x.shape
    depth = ln_g.shape[0]
    n_a = a_w_qkv.shape[0]
    alpha = (2 * depth) ** 0.25
    rope_dim = kv_w_a.shape[1] - kv_norm.shape[0]
    mla_heads = (b_w_uq.shape[2] - kv_w_b.shape[1] + b_w_o.shape[1]) // rope_dim
    h32 = x.reshape(B * S, D)
    hb = h32.astype(BF16)
    kv = k_pe = None
    for l in range(depth):
        if l < n_a:
            lambda_init = 0.8 - 0.6 * math.exp(-0.3 * l)
            y = _diff_attention(hb, a_w_qkv[l], a_lambda[l], a_subln[l], a_w_o[l], lambda_init, B, S)
        else:
            if l == n_a:
                kv, k_pe = _mla_shared_kv(hb, kv_w_a, kv_norm, kv_w_b, mla_heads,
                                          b_w_o.shape[1] // mla_heads, S)
            j = l - n_a
            y = _mla_attention(hb, kv, k_pe, b_w_dq[j], b_q_norm[j], b_w_uq[j], b_w_o[j], mla_heads, B, S)
        h32, hb, hp = _ln_residual(h32, y, ln_g[l, 0], ln_b[l, 0], alpha)
        h32, hb = _moe_ln(h32, hb, hp, l, moe_w_router[l], moe_router_bias[l], moe_w_gate_up, moe_w_down,
                          moe_sh_w_gate_up[l], moe_sh_w_down[l], ln_g[l, 1], ln_b[l, 1], alpha)
    return h32.reshape(B, S, D)
```

```python
import functools
import math

import jax
import jax.numpy as jnp
from jax import lax
from jax.experimental import pallas as pl
from jax.experimental.pallas import tpu as pltpu

F32 = jnp.float32
BF16 = jnp.bfloat16

N_GROUPS = 8
TOPK_GROUPS = 4
TOP_K = 8
ROUTED_SCALE = 2.5
ROPE_THETA = 10000.0
LN_EPS = 1e-5
RMS_EPS = 1e-6
LOG2E = math.log2(math.e)

LANES = 128
V7X_VMEM_LIMIT = 56 * 1024 * 1024
EXPERT_ROWS = 512
COMBINE_TOKENS = 64
ATTN_QUERY_TILE = 1024
ATTN_KEY_TILE = 512


def _params(*sem):
    return pltpu.CompilerParams(dimension_semantics=sem, vmem_limit_bytes=V7X_VMEM_LIMIT)


def _tile(n, pref):
    if n <= pref:
        return n
    t = pref - pref % LANES
    while t >= LANES:
        if n % t == 0:
            return t
        t -= LANES
    return n


def _mm_body(x_ref, w_ref, *rest, epilogue, n_extra):
    extra, o_ref = rest[:n_extra], rest[n_extra]
    acc = jnp.dot(x_ref[...], w_ref[...], preferred_element_type=F32)
    epilogue(acc, o_ref, *extra)


def _matmul(x, w, *, out_dtype, epilogue, extras=(), extra_specs=(), tm=512, tn=1024, name):
    M, K = x.shape
    N = w.shape[1]
    tm, tn = _tile(M, tm), _tile(N, tn)
    return pl.pallas_call(
        functools.partial(_mm_body, epilogue=epilogue, n_extra=len(extras)),
        grid=(N // tn, M // tm),
        in_specs=[pl.BlockSpec((tm, K), lambda j, i: (i, 0)),
                  pl.BlockSpec((K, tn), lambda j, i: (0, j)),
                  *extra_specs],
        out_specs=pl.BlockSpec((tm, tn), lambda j, i: (i, j)),
        out_shape=jax.ShapeDtypeStruct((M, N), out_dtype),
        compiler_params=_params("parallel", "parallel"),
        name=name,
    )(x, w, *extras)


def _ep_plain(acc, o_ref):
    o_ref[...] = acc.astype(o_ref.dtype)


def _ep_rmsnorm(acc, o_ref, g_ref, *, eps):
    ms = jnp.mean(acc * acc, axis=-1, keepdims=True)
    o_ref[...] = (acc * lax.rsqrt(ms + eps) * g_ref[...]).astype(o_ref.dtype)


def _rope_full_group(seg, cos, sin):
    return seg * cos + pltpu.roll(seg, LANES // 2, 1) * sin


def _rope_half_group(seg, c, sa, sb):
    return seg * c + pltpu.roll(seg, 3 * LANES // 4, 1) * sa + pltpu.roll(seg, LANES // 4, 1) * sb


def _ep_qkv(acc, o_ref, tab_ref, *, n_rope_tiles):
    j = pl.program_id(0)

    @pl.when(j < n_rope_tiles)
    def _():
        cos, sin = tab_ref[0], tab_ref[1]
        for g in range(acc.shape[1] // LANES):
            sl = slice(g * LANES, (g + 1) * LANES)
            o_ref[:, sl] = _rope_full_group(acc[:, sl], cos, sin).astype(o_ref.dtype)

    @pl.when(j >= n_rope_tiles)
    def _():
        o_ref[...] = acc.astype(o_ref.dtype)


def _ep_mla_q(acc, o_ref, tab_ref, *, scale):
    c, sa, sb = tab_ref[0], tab_ref[1], tab_ref[2]
    for g in range(acc.shape[1] // LANES):
        sl = slice(g * LANES, (g + 1) * LANES)
        seg = acc[:, sl]
        out = seg * scale if g % 2 == 0 else _rope_half_group(seg, c, sa, sb)
        o_ref[:, sl] = out.astype(o_ref.dtype)


def _ep_kpe(acc, o_ref, tab_ref):
    o_ref[...] = _rope_half_group(acc, tab_ref[0], tab_ref[1], tab_ref[2]).astype(o_ref.dtype)


def _rope_tables_full(S, d, scale):
    inv = ROPE_THETA ** (-jnp.arange(0, d, 2, dtype=F32) / d)
    ang = jnp.arange(S, dtype=F32)[:, None] * inv[None, :]
    cos = jnp.concatenate([jnp.cos(ang), jnp.cos(ang)], axis=-1)
    sin = jnp.concatenate([-jnp.sin(ang), jnp.sin(ang)], axis=-1)
    t = jnp.stack([cos, sin])
    return jnp.stack([t * scale, t])


def _rope_tables_half(S, d, scale):
    inv = ROPE_THETA ** (-jnp.arange(0, d, 2, dtype=F32) / d)
    ang = jnp.arange(S, dtype=F32)[:, None] * inv[None, :]
    cos, sin = jnp.cos(ang), jnp.sin(ang)
    z = jnp.zeros_like(cos)
    pad = jnp.zeros((S, LANES - d), F32)
    c = jnp.concatenate([cos, cos, pad], axis=-1)
    sa = jnp.concatenate([-sin, z, pad], axis=-1)
    sb = jnp.concatenate([z, sin, pad], axis=-1)
    t = jnp.stack([c, sa, sb])
    return jnp.stack([t * scale, t])


def _pack_halves(x):
    half = x.shape[1] // 2
    lo = lax.bitcast_convert_type(x[:, :half].astype(BF16).astype(F32), jnp.uint32)
    hi = lax.bitcast_convert_type(x[:, half:].astype(BF16).astype(F32), jnp.uint32)
    return (lo >> 16) | (hi & jnp.uint32(0xFFFF0000))


def _unpack_halves(w):
    lo = lax.bitcast_convert_type(w << 16, F32)
    hi = lax.bitcast_convert_type(w & jnp.uint32(0xFFFF0000), F32)
    return lo, hi


def _ln_body(h_ref, y_ref, g_ref, b_ref, of_ref, ob_ref, op_ref, *, alpha):
    z = alpha * h_ref[...] + y_ref[...]
    mu = jnp.mean(z, axis=-1, keepdims=True)
    zc = z - mu
    var = jnp.mean(zc * zc, axis=-1, keepdims=True)
    o = zc * lax.rsqrt(var + LN_EPS) * g_ref[...] + b_ref[...]
    of_ref[...] = o
    ob_ref[...] = o.astype(BF16)
    op_ref[...] = _pack_halves(o)


def _ln_residual(h, y, g, b, alpha):
    N, D = h.shape
    tm = _tile(N, 256)
    row = pl.BlockSpec((tm, D), lambda i: (i, 0))
    half = pl.BlockSpec((tm, D // 2), lambda i: (i, 0))
    vec = pl.BlockSpec((1, D), lambda i: (0, 0))
    return pl.pallas_call(
        functools.partial(_ln_body, alpha=alpha),
        grid=(N // tm,),
        in_specs=[row, row, vec, vec],
        out_specs=[row, row, half],
        out_shape=[jax.ShapeDtypeStruct((N, D), F32), jax.ShapeDtypeStruct((N, D), BF16),
                   jax.ShapeDtypeStruct((N, D // 2), jnp.uint32)],
        compiler_params=_params("parallel"),
        name="ln_residual",
    )(h, y, g.reshape(1, D), b.reshape(1, D))


def _attn_body(*refs, n_maps, dk, tq, tk, has_kpe, finalize):
    q_ref, k_ref, v_ref = refs[:3]
    pos = 3
    kpe_ref = None
    if has_kpe:
        kpe_ref, pos = refs[3], 4
    fin_refs = refs[pos:-5]
    o_ref, m_sc, l_sc, acc_sc, s_sc = refs[-5:]
    qi = pl.program_id(2)
    ratio = tq // tk

    m_sc[...] = jnp.full(m_sc.shape, -jnp.inf, F32)
    l_sc[...] = jnp.zeros(l_sc.shape, F32)
    acc_sc[...] = jnp.zeros(acc_sc.shape, F32)

    def scores(ki, m, r0):
        off = pl.multiple_of(ki * tk, tk)
        k = k_ref[pl.ds(off, tk), :]
        if has_kpe:
            k = jnp.concatenate([k, kpe_ref[pl.ds(off, tk), :]], axis=-1)
        return lax.dot_general(q_ref[r0:, m * dk:(m + 1) * dk], k[:, m * dk:(m + 1) * dk],
                               (((1,), (1,)), ((), ())), preferred_element_type=F32)

    def accumulate(ki, m, s, r0):
        v = v_ref[pl.ds(pl.multiple_of(ki * tk, tk), tk), :]
        m_old = m_sc[m, r0:]
        m_new = jnp.maximum(m_old, jnp.max(s, axis=-1, keepdims=True))
        p = jnp.exp2(s - _lane_tile(m_new, tk))
        alpha = jnp.exp2(m_old - m_new)
        l_sc[m, r0:] = alpha * l_sc[m, r0:] + jnp.sum(p, axis=-1, keepdims=True)
        acc_sc[m, r0:] = (_lane_tile(alpha, acc_sc.shape[2]) * acc_sc[m, r0:]
                          + jnp.dot(p.astype(BF16), v, preferred_element_type=F32))
        m_sc[m, r0:] = m_new

    def causal(s):
        row = lax.broadcasted_iota(jnp.int32, s.shape, 0)
        col = lax.broadcasted_iota(jnp.int32, s.shape, 1)
        return jnp.where(col <= row, s, -jnp.inf)

    for m in range(n_maps):
        s_sc[m] = scores(0, m, 0)

    def full_step(ki, carry):
        for m in range(n_maps):
            s_next = scores(ki + 1, m, 0)
            accumulate(ki, m, s_sc[m], 0)
            s_sc[m] = s_next
        return carry

    n_full = ratio * qi
    lax.fori_loop(0, n_full, full_step, 0)
    for m in range(n_maps):
        accumulate(n_full, m, causal(s_sc[m]), 0)
    for j in range(1, ratio):
        for m in range(n_maps):
            accumulate(n_full + j, m, causal(scores(n_full + j, m, j * tk)), j * tk)
    finalize(o_ref, l_sc, acc_sc, *fin_refs)


def _lane_tile(x, width):
    return jnp.tile(x, (1, width // LANES))


def _fin_plain(o_ref, l_sc, acc_sc):
    o_ref[...] = (acc_sc[0] / _lane_tile(l_sc[0], acc_sc.shape[2])).astype(o_ref.dtype)


def _fin_diff(o_ref, l_sc, acc_sc, lam_ref, subln_ref, *, lambda_init):
    lam = lam_ref[...]
    lam_full = (jnp.exp(jnp.sum(lam[0:1] * lam[1:2], axis=-1, keepdims=True))
                - jnp.exp(jnp.sum(lam[2:3] * lam[3:4], axis=-1, keepdims=True)) + lambda_init)
    dv = acc_sc.shape[2]
    o = acc_sc[0] / _lane_tile(l_sc[0], dv) - lam_full * (acc_sc[1] / _lane_tile(l_sc[1], dv))
    ms = jnp.mean(o * o, axis=-1, keepdims=True)
    o = o * lax.rsqrt(ms + LN_EPS) * subln_ref[...] * (1.0 - lambda_init)
    o_ref[...] = o.astype(o_ref.dtype)


def _attention(q, k, v, *, B, S, H, n_maps, dk, dv, q_col0, k_col0, v_col0, k_width,
               kpe=None, finalize, fin_args=(), fin_specs=(), name):
    tk = _tile(S, ATTN_KEY_TILE)
    T = _tile(S, ATTN_QUERY_TILE)
    assert T % tk == 0
    nq = S // T
    in_specs = [pl.BlockSpec((T, n_maps * dk), lambda b, h, i: (b * nq + i, q_col0 + h)),
                pl.BlockSpec((S, k_width), lambda b, h, i: (b, k_col0 + h)),
                pl.BlockSpec((S, dv), lambda b, h, i: (b, v_col0 + h))]
    args = [q, k, v]
    if kpe is not None:
        in_specs.append(pl.BlockSpec((S, kpe.shape[1]), lambda b, h, i: (b, 0)))
        args.append(kpe)
    return pl.pallas_call(
        functools.partial(_attn_body, n_maps=n_maps, dk=dk, tq=T, tk=tk, has_kpe=kpe is not None,
                          finalize=finalize),
        grid=(B, H, nq),
        in_specs=in_specs + list(fin_specs),
        out_specs=pl.BlockSpec((T, dv), lambda b, h, i: (b * nq + i, h)),
        out_shape=jax.ShapeDtypeStruct((B * S, H * dv), BF16),
        scratch_shapes=[pltpu.VMEM((n_maps, T, LANES), F32), pltpu.VMEM((n_maps, T, LANES), F32),
                        pltpu.VMEM((n_maps, T, dv), F32), pltpu.VMEM((n_maps, T, tk), F32)],
        compiler_params=_params("parallel", "parallel", "parallel"),
        name=name,
    )(*args, *fin_args)


def _shared_body(x_ref, wg_ref, wu_ref, wd_ref, o_ref):
    x = x_ref[...]
    g = jnp.dot(x, wg_ref[...], preferred_element_type=F32)
    u = jnp.dot(x, wu_ref[...], preferred_element_type=F32)
    a = (g * jax.nn.sigmoid(g) * u).astype(BF16)
    o_ref[...] = jnp.dot(a, wd_ref[...], preferred_element_type=F32)


def _shared_expert(xb, wg, wu, wd):
    N, D = xb.shape
    F = wg.shape[1]
    tm = _tile(N, 512)
    full = lambda shape: pl.BlockSpec(shape, lambda i: (0, 0))
    return pl.pallas_call(
        _shared_body,
        grid=(N // tm,),
        in_specs=[pl.BlockSpec((tm, D), lambda i: (i, 0)), full((D, F)), full((D, F)), full((F, D))],
        out_specs=pl.BlockSpec((tm, D), lambda i: (i, 0)),
        out_shape=jax.ShapeDtypeStruct((N, D), F32),
        compiler_params=_params("parallel"),
        name="moe_shared",
    )(xb, wg, wu, wd)


def _expert_body(be_ref, nu_ref, tok_ref, tokn_ref, x_hbm, wgu_ref, wd_ref, y_ref,
                 xbuf, sem, wg_sc, wu_sc, wd_sc, *, rows, ffn, n_chunk):
    b = pl.program_id(0)
    last = pl.num_programs(0) - 1
    nu = nu_ref[0]
    slot = b % 2

    def row_copy(idx_ref, r, s):
        return pltpu.make_async_copy(x_hbm.at[pl.ds(idx_ref[0, 0, r], 1), :],
                                     xbuf.at[s, pl.ds(r, 1), :], sem.at[s])

    def wait_gather(s):
        pltpu.make_async_copy(x_hbm.at[pl.ds(0, rows), :], xbuf.at[s], sem.at[s]).wait()

    @pl.when(b == 0)
    def _():
        def body(r, carry):
            row_copy(tok_ref, r, 0).start()
            return carry
        lax.fori_loop(0, rows, body, 0, unroll=8)

    @pl.when(b <= nu)
    def _():
        wait_gather(slot)

    @pl.when(b < nu)
    def _():
        @pl.when((b == 0) | (be_ref[b] != be_ref[jnp.maximum(b - 1, 0)]))
        def _():
            wg_sc[...] = wgu_ref[:, :ffn].astype(BF16)
            wu_sc[...] = wgu_ref[:, ffn:].astype(BF16)
            wd_sc[...] = wd_ref[...].astype(BF16)

        half = xbuf.shape[2]
        lo, hi = _unpack_halves(xbuf[slot])
        lo, hi = lo.astype(BF16), hi.astype(BF16)
        g = (jnp.dot(lo, wg_sc[:half, :], preferred_element_type=F32)
             + jnp.dot(hi, wg_sc[half:, :], preferred_element_type=F32))
        u = (jnp.dot(lo, wu_sc[:half, :], preferred_element_type=F32)
             + jnp.dot(hi, wu_sc[half:, :], preferred_element_type=F32))
        for r in range(rows):
            row_copy(tokn_ref, r, 1 - slot).start(priority=r % 2)
        a = (g * jax.nn.sigmoid(g) * u).astype(BF16)
        for c in range(half // n_chunk):
            n0 = c * n_chunk
            y_lo = jnp.dot(a, wd_sc[:, n0:n0 + n_chunk], preferred_element_type=F32)
            y_hi = jnp.dot(a, wd_sc[:, half + n0:half + n0 + n_chunk], preferred_element_type=F32)
            y_ref[:, n0:n0 + n_chunk] = _pack_halves(jnp.concatenate([y_lo, y_hi], axis=1))

    @pl.when(b >= nu)
    def _():
        y_ref[...] = jnp.zeros(y_ref.shape, y_ref.dtype)

    @pl.when((b == last) & (b < nu))
    def _():
        wait_gather(1 - slot)


def _routed_experts(hp, row_tok, block_e, n_used, w_gate_up, w_down, layer):
    N, half = hp.shape
    D = 2 * half
    _, E, _, F2 = w_gate_up.shape
    F = F2 // 2
    C = EXPERT_ROWS
    n_blocks = block_e.shape[0]
    n_chunk = _tile(half, 512)
    tok3 = row_tok.reshape(n_blocks, 1, C)
    smem_blk = lambda fn: pl.BlockSpec((1, 1, C), fn, memory_space=pltpu.SMEM)
    grid_spec = pltpu.PrefetchScalarGridSpec(
        num_scalar_prefetch=2,
        grid=(n_blocks,),
        in_specs=[smem_blk(lambda b, be, nu: (b, 0, 0)),
                  smem_blk(lambda b, be, nu: (jnp.minimum(b + 1, n_blocks - 1), 0, 0)),
                  pl.BlockSpec(memory_space=pl.ANY),
                  pl.BlockSpec((None, None, D, F2), lambda b, be, nu: (layer, be[b], 0, 0)),
                  pl.BlockSpec((None, None, F, D), lambda b, be, nu: (layer, be[b], 0, 0))],
        out_specs=pl.BlockSpec((C, half), lambda b, be, nu: (b, 0)),
        scratch_shapes=[pltpu.VMEM((2, C, half), jnp.uint32), pltpu.SemaphoreType.DMA((2,)),
                        pltpu.VMEM((D, F), BF16), pltpu.VMEM((D, F), BF16), pltpu.VMEM((F, D), BF16)],
    )
    return pl.pallas_call(
        functools.partial(_expert_body, rows=C, ffn=F, n_chunk=n_chunk),
        grid_spec=grid_spec,
        out_shape=jax.ShapeDtypeStruct((n_blocks * C, half), jnp.uint32),
        compiler_params=_params("arbitrary"),
        name="moe_experts",
    )(block_e, n_used, tok3, tok3, hp, w_gate_up, w_down)


def _combine_body(pos_ref, posn_ref, y_hbm, gate_ref, h_ref, sh_ref, g_ref, b_ref, of_ref, ob_ref,
                  ybuf, sem, *, tokens, top_k, alpha):
    i = pl.program_id(0)
    last = pl.num_programs(0) - 1
    slot = i % 2

    def row_copy(idx_ref, t, k, s):
        return pltpu.make_async_copy(y_hbm.at[pl.ds(idx_ref[0, 0, t * top_k + k], 1), :],
                                     ybuf.at[s, k, pl.ds(t, 1), :], sem.at[s])

    def wait_gather(s):
        for k in range(top_k):
            pltpu.make_async_copy(y_hbm.at[pl.ds(0, tokens), :], ybuf.at[s, k], sem.at[s]).wait()

    @pl.when(i == 0)
    def _():
        def body(t, carry):
            for k in range(top_k):
                row_copy(pos_ref, t, k, 0).start()
            return carry
        lax.fori_loop(0, tokens, body, 0)

    wait_gather(slot)

    r_lo = r_hi = None
    for k in range(top_k):
        for t in range(tokens):
            row_copy(posn_ref, t, k, 1 - slot).start(priority=t % 2)
        lo, hi = _unpack_halves(ybuf[slot, k])
        gate = gate_ref[k]
        r_lo = gate * lo if r_lo is None else r_lo + gate * lo
        r_hi = gate * hi if r_hi is None else r_hi + gate * hi

    @pl.when(i == last)
    def _():
        wait_gather(1 - slot)

    routed = jnp.concatenate([r_lo, r_hi], axis=1)
    z = alpha * h_ref[...] + (routed + sh_ref[...])
    mu = jnp.mean(z, axis=-1, keepdims=True)
    zc = z - mu
    var = jnp.mean(zc * zc, axis=-1, keepdims=True)
    o = zc * lax.rsqrt(var + LN_EPS) * g_ref[...] + b_ref[...]
    of_ref[...] = o
    ob_ref[...] = o.astype(BF16)


def _combine_ln(y_sorted, dest, gates, h32, shared, g, b, alpha):
    N, D = h32.shape
    K = dest.shape[0]
    T = COMBINE_TOKENS
    assert N % T == 0
    n_tiles = N // T
    pos3 = dest.T.reshape(n_tiles, 1, T * K)
    smem_blk = lambda fn: pl.BlockSpec((1, 1, T * K), fn, memory_space=pltpu.SMEM)
    row = pl.BlockSpec((T, D), lambda i: (i, 0))
    vec = pl.BlockSpec((1, D), lambda i: (0, 0))
    return pl.pallas_call(
        functools.partial(_combine_body, tokens=T, top_k=K, alpha=alpha),
        grid=(n_tiles,),
        in_specs=[smem_blk(lambda i: (i, 0, 0)),
                  smem_blk(lambda i: (jnp.minimum(i + 1, n_tiles - 1), 0, 0)),
                  pl.BlockSpec(memory_space=pl.ANY),
                  pl.BlockSpec((K, T, 1), lambda i: (0, i, 0)), row, row, vec, vec],
        out_specs=[row, row],
        out_shape=[jax.ShapeDtypeStruct((N, D), F32), jax.ShapeDtypeStruct((N, D), BF16)],
        scratch_shapes=[pltpu.VMEM((2, K, T, D // 2), jnp.uint32), pltpu.SemaphoreType.DMA((2,))],
        compiler_params=_params("arbitrary"),
        name="moe_combine_ln",
    )(pos3, pos3, y_sorted, gates.reshape(K, N, 1), h32, shared, g.reshape(1, D), b.reshape(1, D))


def _router_body(x_ref, w_ref, bias_ref, ek_ref, rk_ref, gk_ref, cnt_ref, carry_sc, tri_sc,
                 *, n_groups, topk_groups, top_k, scale):
    i = pl.program_id(0)
    E, T = w_ref.shape[0], x_ref.shape[0]
    G, Eg = n_groups, w_ref.shape[0] // n_groups
    neg = -jnp.inf

    @pl.when(i == 0)
    def _():
        carry_sc[...] = jnp.zeros(carry_sc.shape, F32)
        r = lax.broadcasted_iota(jnp.int32, (T, T), 0)
        c = lax.broadcasted_iota(jnp.int32, (T, T), 1)
        tri_sc[...] = jnp.where(r < c, 1.0, 0.0).astype(BF16)

    logits = lax.dot_general(w_ref[...], x_ref[...], (((1,), (1,)), ((), ())), preferred_element_type=F32)
    scores = jax.nn.sigmoid(logits)
    choice = scores + bias_ref[...]
    sub = lax.broadcasted_iota(jnp.int32, (Eg, T), 0)
    sc_g = [scores[g * Eg:(g + 1) * Eg] for g in range(G)]
    ch_g = [choice[g * Eg:(g + 1) * Eg] for g in range(G)]

    def first_index(hit, idx, size):
        return jnp.min(jnp.where(hit, idx, size), axis=0, keepdims=True)

    gs = []
    for x in ch_g:
        m1 = jnp.max(x, axis=0, keepdims=True)
        first = first_index(x == m1, sub, Eg)
        m2 = jnp.max(jnp.where(sub == first, neg, x), axis=0, keepdims=True)
        gs.append(m1 + m2)
    gs = jnp.concatenate(gs, axis=0)
    gid = lax.broadcasted_iota(jnp.int32, (G, T), 0)
    g_on = jnp.zeros((G, T), F32)
    for _ in range(topk_groups):
        m = jnp.max(gs, axis=0, keepdims=True)
        pick = gid == first_index(gs == m, gid, G)
        g_on = jnp.where(pick, 1.0, g_on)
        gs = jnp.where(pick, neg, gs)
    mk = [jnp.where(g_on[g:g + 1] > 0.0, ch_g[g], neg) for g in range(G)]

    eid = [sub + g * Eg for g in range(G)]
    sel = [jnp.zeros((Eg, T), F32) for _ in range(G)]
    e_k, s_k = [], []
    for _ in range(top_k):
        m = mk[0]
        for g in range(1, G):
            m = jnp.maximum(m, mk[g])
        m = jnp.max(m, axis=0, keepdims=True)
        cand = jnp.where(mk[0] == m, eid[0], E)
        for g in range(1, G):
            cand = jnp.minimum(cand, jnp.where(mk[g] == m, eid[g], E))
        first = jnp.min(cand, axis=0, keepdims=True)
        s = jnp.zeros((Eg, T), F32)
        for g in range(G):
            pick = eid[g] == first
            s = s + jnp.where(pick, sc_g[g], 0.0)
            mk[g] = jnp.where(pick, neg, mk[g])
            sel[g] = jnp.where(pick, 1.0, sel[g])
        e_k.append(first)
        s_k.append(jnp.sum(s, axis=0, keepdims=True))
    denom = s_k[0]
    for s in s_k[1:]:
        denom = denom + s
    gates = [s / denom * scale for s in s_k]

    sel_b = jnp.concatenate(sel, axis=0).astype(BF16)
    rank = (jnp.dot(sel_b, tri_sc[...], preferred_element_type=F32)
            + _lane_tile(carry_sc[...], T))
    total = carry_sc[...] + jnp.dot(sel_b, jnp.ones((T, LANES), BF16), preferred_element_type=F32)
    carry_sc[...] = total
    cnt_ref[...] = total
    r_k = []
    for first in e_k:
        r = jnp.zeros((Eg, T), F32)
        for g in range(G):
            r = r + jnp.where(eid[g] == first, rank[g * Eg:(g + 1) * Eg], 0.0)
        r_k.append(jnp.sum(r, axis=0, keepdims=True))
    ek_ref[...] = jnp.concatenate(e_k, axis=0)
    rk_ref[...] = jnp.concatenate(r_k, axis=0).astype(jnp.int32)
    gk_ref[...] = jnp.concatenate(gates, axis=0)


def _router(hb, w_router, bias):
    N, D = hb.shape
    E = w_router.shape[1]
    T = _tile(N, 512)
    assert E // N_GROUPS == 8 and T % LANES == 0
    kn = pl.BlockSpec((TOP_K, T), lambda i: (0, i))
    ek, rk, gk, cnt = pl.pallas_call(
        functools.partial(_router_body, n_groups=N_GROUPS, topk_groups=TOPK_GROUPS, top_k=TOP_K,
                          scale=ROUTED_SCALE),
        grid=(N // T,),
        in_specs=[pl.BlockSpec((T, D), lambda i: (i, 0)),
                  pl.BlockSpec((E, D), lambda i: (0, 0)),
                  pl.BlockSpec((E, 1), lambda i: (0, 0))],
        out_specs=[kn, kn, kn, pl.BlockSpec((E, LANES), lambda i: (0, 0))],
        out_shape=[jax.ShapeDtypeStruct((TOP_K, N), jnp.int32), jax.ShapeDtypeStruct((TOP_K, N), jnp.int32),
                   jax.ShapeDtypeStruct((TOP_K, N), F32), jax.ShapeDtypeStruct((E, LANES), F32)],
        scratch_shapes=[pltpu.VMEM((E, LANES), F32), pltpu.VMEM((T, T), BF16)],
        compiler_params=_params("arbitrary"),
        name="moe_router",
    )(hb, w_router.T.astype(BF16), bias.astype(F32).reshape(E, 1))
    return ek, rk, gk, cnt[:, 0].astype(jnp.int32)


def _row_layout(ek, rk, counts):
    K, N = ek.shape
    E = counts.shape[0]
    C = EXPERT_ROWS
    n_blocks = -(-(N * K) // C) + E
    padded = (counts + C - 1) // C * C
    pad_end = jnp.cumsum(padded)
    expert = jnp.arange(E, dtype=jnp.int32)[:, None, None]
    dest = jnp.sum(jnp.where(ek[None] == expert, (pad_end - padded)[:, None, None], 0), axis=0) + rk
    token = jnp.broadcast_to(jnp.arange(N, dtype=jnp.int32)[None, :], (K, N))
    row_tok = jnp.zeros((n_blocks * C,), jnp.int32).at[dest.reshape(-1)].set(
        token.reshape(-1), unique_indices=True)
    block_start = jnp.arange(n_blocks, dtype=jnp.int32) * C
    block_e = jnp.minimum(jnp.sum(pad_end[None, :] <= block_start[:, None], axis=1), E - 1).astype(jnp.int32)
    n_used = (pad_end[-1:] // C).astype(jnp.int32)
    return dest, row_tok, block_e, n_used


def _moe_ln(h32, hb, hp, layer, w_router, router_bias, w_gate_up, w_down, sh_w_gate_up, sh_w_down, g, b, alpha):
    F = sh_w_gate_up.shape[1] // 2
    ek, rk, gates, counts = _router(hb, w_router, router_bias)
    dest, row_tok, block_e, n_used = _row_layout(ek, rk, counts)
    y_sorted = _routed_experts(hp, row_tok, block_e, n_used, w_gate_up, w_down, layer)
    shared = _shared_expert(hb, sh_w_gate_up[:, :F].astype(BF16), sh_w_gate_up[:, F:].astype(BF16),
                            sh_w_down.astype(BF16))
    return _combine_ln(y_sorted, dest, gates, h32, shared, g, b, alpha)


def _diff_attention(hb, w_qkv, lam, subln, w_o, lambda_init, B, S):
    N, D = hb.shape
    d = lam.shape[1]
    H = D // (2 * d)
    tn = _tile(D, 1024)
    tm = _tile(S, 1024)
    tabs = _rope_tables_full(S, d, d ** -0.5 * LOG2E)
    n_rope_tiles = 2 * D // tn
    nsb = S // tm
    qkv = _matmul(
        hb, w_qkv.astype(BF16), out_dtype=BF16, tm=tm, tn=tn,
        epilogue=functools.partial(_ep_qkv, n_rope_tiles=n_rope_tiles),
        extras=(tabs,),
        extra_specs=(pl.BlockSpec((None, 2, tm, d),
                                  lambda j, i: (jnp.minimum(j // (n_rope_tiles // 2), 1), 0, i % nsb, 0)),),
        name="diff_qkv")
    full = lambda shape: pl.BlockSpec(shape, lambda b, h, i: (0, 0))
    o = _attention(qkv, qkv, qkv, B=B, S=S, H=H, n_maps=2, dk=d, dv=2 * d,
                   q_col0=0, k_col0=H, v_col0=2 * H, k_width=2 * d,
                   finalize=functools.partial(_fin_diff, lambda_init=lambda_init),
                   fin_args=(lam, subln.reshape(1, 2 * d)),
                   fin_specs=(full((4, d)), full((1, 2 * d))), name="diff_attn")
    return _matmul(o, w_o.astype(BF16), out_dtype=F32, epilogue=_ep_plain, name="diff_out")


def _mla_shared_kv(hb, w_a, kv_norm, w_b, H, vdim, S):
    N, D = hb.shape
    R = kv_norm.shape[0]
    rope = w_a.shape[1] - R
    hw = w_b.shape[1] // H
    nope = hw - vdim
    assert nope == vdim == LANES and rope <= LANES // 2
    tm = _tile(S, 512)
    nsb = S // tm
    c_kv = _matmul(hb, w_a[:, :R].astype(BF16), out_dtype=BF16, tm=tm, tn=R,
                   epilogue=functools.partial(_ep_rmsnorm, eps=RMS_EPS),
                   extras=(kv_norm.reshape(1, R),),
                   extra_specs=(pl.BlockSpec((1, R), lambda j, i: (0, 0)),), name="mla_kv_a")
    w_pe = jnp.pad(w_a[:, R:], ((0, 0), (0, LANES - rope))).astype(BF16)
    tabs = _rope_tables_half(S, rope, 1.0)
    k_pe = _matmul(hb, w_pe, out_dtype=BF16, tm=tm, tn=LANES, epilogue=_ep_kpe, extras=(tabs,),
                   extra_specs=(pl.BlockSpec((None, 3, tm, LANES), lambda j, i: (1, 0, i % nsb, 0)),),
                   name="mla_k_pe")
    w_b3 = w_b.reshape(R, H, hw)
    w_b2 = jnp.concatenate([w_b3[:, :, :nope].reshape(R, H * nope),
                            w_b3[:, :, nope:].reshape(R, H * (hw - nope))], axis=1).astype(BF16)
    kv = _matmul(c_kv, w_b2, out_dtype=BF16, epilogue=_ep_plain, name="mla_kv_b")
    return kv, k_pe


def _mla_attention(hb, kv, k_pe, w_dq, q_norm, w_uq, w_o, H, B, S):
    N, D = hb.shape
    Qr = q_norm.shape[0]
    hq = w_uq.shape[1] // H
    nope = kv.shape[1] // (2 * H)
    rope = hq - nope
    tm = _tile(S, 512)
    nsb = S // tm
    c_q = _matmul(hb, w_dq.astype(BF16), out_dtype=BF16, tm=tm, tn=Qr,
                  epilogue=functools.partial(_ep_rmsnorm, eps=RMS_EPS),
                  extras=(q_norm.reshape(1, Qr),),
                  extra_specs=(pl.BlockSpec((1, Qr), lambda j, i: (0, 0)),), name="mla_dq")
    scale = float(hq) ** -0.5 * LOG2E
    w_q = jnp.pad(w_uq.reshape(Qr, H, hq), ((0, 0), (0, 0), (0, 2 * LANES - hq)))
    w_q = w_q.reshape(Qr, H * 2 * LANES).astype(BF16)
    tabs = _rope_tables_half(S, rope, scale)
    q = _matmul(c_q, w_q, out_dtype=BF16, tm=tm,
                epilogue=functools.partial(_ep_mla_q, scale=scale), extras=(tabs,),
                extra_specs=(pl.BlockSpec((None, 3, tm, LANES), lambda j, i: (0, 0, i % nsb, 0)),),
                name="mla_uq")
    o = _attention(q, kv, kv, B=B, S=S, H=H, n_maps=1, dk=2 * LANES, dv=nope,
                   q_col0=0, k_col0=0, v_col0=H, k_width=nope, kpe=k_pe,
                   finalize=_fin_plain, name="mla_attn")
    return _matmul(o, w_o.astype(BF16), out_dtype=F32, epilogue=_ep_plain, name="mla_out")


def kernel(x, ln_g, ln_b, a_w_qkv, a_lambda, a_subln, a_w_o, kv_w_a, kv_norm, kv_w_b, b_w_dq, b_q_norm, b_w_uq, b_w_o, moe_w_router, moe_router_bias, moe_w_gate_up, moe_w_down, moe_sh_w_gate_up, moe_sh_w_down):
    B, S, D = x.shape
    depth = ln_g.shape[0]
    n_a = a_w_qkv.shape[0]
    alpha = (2 * depth) ** 0.25
    rope_dim = kv_w_a.shape[1] - kv_norm.shape[0]
    mla_heads = (b_w_uq.shape[2] - kv_w_b.shape[1] + b_w_o.shape[1]) // rope_dim
    h32 = x.reshape(B * S, D)
    hb = h32.astype(BF16)
    kv = k_pe = None
    for l in range(depth):
        if l < n_a:
            lambda_init = 0.8 - 0.6 * math.exp(-0.3 * l)
            y = _diff_attention(hb, a_w_qkv[l], a_lambda[l], a_subln[l], a_w_o[l], lambda_init, B, S)
        else:
            if l == n_a:
                kv, k_pe = _mla_shared_kv(hb, kv_w_a, kv_norm, kv_w_b, mla_heads,
                                          b_w_o.shape[1] // mla_heads, S)
            j = l - n_a
            y = _mla_attention(hb, kv, k_pe, b_w_dq[j], b_q_norm[j], b_w_uq[j], b_w_o[j], mla_heads, B, S)
        h32, hb, hp = _ln_residual(h32, y, ln_g[l, 0], ln_b[l, 0], alpha)
        h32, hb = _moe_ln(h32, hb, hp, l, moe_w_router[l], moe_router_bias[l], moe_w_gate_up, moe_w_down,
                          moe_sh_w_gate_up[l], moe_sh_w_down[l], ln_g[l, 1], ln_b[l, 1], alpha)
    return h32.reshape(B, S, D)
```

```python
import functools
import math

import jax
import jax.numpy as jnp
from jax import lax
from jax.experimental import pallas as pl
from jax.experimental.pallas import tpu as pltpu

F32 = jnp.float32
BF16 = jnp.bfloat16

N_GROUPS = 8
TOPK_GROUPS = 4
TOP_K = 8
ROUTED_SCALE = 2.5
ROPE_THETA = 10000.0
LN_EPS = 1e-5
RMS_EPS = 1e-6
LOG2E = math.log2(math.e)

LANES = 128
V7X_VMEM_LIMIT = 56 * 1024 * 1024
EXPERT_ROWS = 512
ROW_ISSUE_STRIDE = 67
COMBINE_TOKENS = 64
ATTN_QUERY_TILE = 1024
ATTN_KEY_TILE = 512


def _params(*sem):
    return pltpu.CompilerParams(dimension_semantics=sem, vmem_limit_bytes=V7X_VMEM_LIMIT)


def _tile(n, pref):
    if n <= pref:
        return n
    t = pref - pref % LANES
    while t >= LANES:
        if n % t == 0:
            return t
        t -= LANES
    return n


def _mm_body(x_ref, w_ref, *rest, epilogue, n_extra):
    extra, o_ref = rest[:n_extra], rest[n_extra]
    acc = jnp.dot(x_ref[...], w_ref[...], preferred_element_type=F32)
    epilogue(acc, o_ref, *extra)


def _matmul(x, w, *, out_dtype, epilogue, extras=(), extra_specs=(), tm=512, tn=1024, name):
    M, K = x.shape
    N = w.shape[1]
    tm, tn = _tile(M, tm), _tile(N, tn)
    return pl.pallas_call(
        functools.partial(_mm_body, epilogue=epilogue, n_extra=len(extras)),
        grid=(N // tn, M // tm),
        in_specs=[pl.BlockSpec((tm, K), lambda j, i: (i, 0)),
                  pl.BlockSpec((K, tn), lambda j, i: (0, j)),
                  *extra_specs],
        out_specs=pl.BlockSpec((tm, tn), lambda j, i: (i, j)),
        out_shape=jax.ShapeDtypeStruct((M, N), out_dtype),
        compiler_params=_params("parallel", "parallel"),
        name=name,
    )(x, w, *extras)


def _ep_plain(acc, o_ref):
    o_ref[...] = acc.astype(o_ref.dtype)


def _ep_rmsnorm(acc, o_ref, g_ref, *, eps):
    ms = jnp.mean(acc * acc, axis=-1, keepdims=True)
    o_ref[...] = (acc * lax.rsqrt(ms + eps) * g_ref[...]).astype(o_ref.dtype)


def _rope_full_group(seg, cos, sin):
    return seg * cos + pltpu.roll(seg, LANES // 2, 1) * sin


def _rope_half_group(seg, c, sa, sb):
    return seg * c + pltpu.roll(seg, 3 * LANES // 4, 1) * sa + pltpu.roll(seg, LANES // 4, 1) * sb


def _ep_qkv(acc, o_ref, tab_ref, *, n_rope_tiles):
    j = pl.program_id(0)

    @pl.when(j < n_rope_tiles)
    def _():
        cos, sin = tab_ref[0], tab_ref[1]
        for g in range(acc.shape[1] // LANES):
            sl = slice(g * LANES, (g + 1) * LANES)
            o_ref[:, sl] = _rope_full_group(acc[:, sl], cos, sin).astype(o_ref.dtype)

    @pl.when(j >= n_rope_tiles)
    def _():
        o_ref[...] = acc.astype(o_ref.dtype)


def _ep_mla_q(acc, o_ref, tab_ref, *, scale):
    c, sa, sb = tab_ref[0], tab_ref[1], tab_ref[2]
    for g in range(acc.shape[1] // LANES):
        sl = slice(g * LANES, (g + 1) * LANES)
        seg = acc[:, sl]
        out = seg * scale if g % 2 == 0 else _rope_half_group(seg, c, sa, sb)
        o_ref[:, sl] = out.astype(o_ref.dtype)


def _ep_kpe(acc, o_ref, tab_ref):
    o_ref[...] = _rope_half_group(acc, tab_ref[0], tab_ref[1], tab_ref[2]).astype(o_ref.dtype)


def _rope_tables_full(S, d, scale):
    inv = ROPE_THETA ** (-jnp.arange(0, d, 2, dtype=F32) / d)
    ang = jnp.arange(S, dtype=F32)[:, None] * inv[None, :]
    cos = jnp.concatenate([jnp.cos(ang), jnp.cos(ang)], axis=-1)
    sin = jnp.concatenate([-jnp.sin(ang), jnp.sin(ang)], axis=-1)
    t = jnp.stack([cos, sin])
    return jnp.stack([t * scale, t])


def _rope_tables_half(S, d, scale):
    inv = ROPE_THETA ** (-jnp.arange(0, d, 2, dtype=F32) / d)
    ang = jnp.arange(S, dtype=F32)[:, None] * inv[None, :]
    cos, sin = jnp.cos(ang), jnp.sin(ang)
    z = jnp.zeros_like(cos)
    pad = jnp.zeros((S, LANES - d), F32)
    c = jnp.concatenate([cos, cos, pad], axis=-1)
    sa = jnp.concatenate([-sin, z, pad], axis=-1)
    sb = jnp.concatenate([z, sin, pad], axis=-1)
    t = jnp.stack([c, sa, sb])
    return jnp.stack([t * scale, t])


def _pack_halves(x):
    half = x.shape[1] // 2
    lo = lax.bitcast_convert_type(x[:, :half].astype(BF16).astype(F32), jnp.uint32)
    hi = lax.bitcast_convert_type(x[:, half:].astype(BF16).astype(F32), jnp.uint32)
    return (lo >> 16) | (hi & jnp.uint32(0xFFFF0000))


def _unpack_halves(w):
    lo = lax.bitcast_convert_type(w << 16, F32)
    hi = lax.bitcast_convert_type(w & jnp.uint32(0xFFFF0000), F32)
    return lo, hi


def _ln_body(h_ref, y_ref, g_ref, b_ref, of_ref, ob_ref, op_ref, *, alpha):
    z = alpha * h_ref[...] + y_ref[...].astype(F32)
    mu = jnp.mean(z, axis=-1, keepdims=True)
    zc = z - mu
    var = jnp.mean(zc * zc, axis=-1, keepdims=True)
    o = zc * lax.rsqrt(var + LN_EPS) * g_ref[...] + b_ref[...]
    of_ref[...] = o
    ob_ref[...] = o.astype(BF16)
    op_ref[...] = _pack_halves(o)


def _ln_residual(h, y, g, b, alpha):
    N, D = h.shape
    tm = _tile(N, 256)
    row = pl.BlockSpec((tm, D), lambda i: (i, 0))
    half = pl.BlockSpec((tm, D // 2), lambda i: (i, 0))
    vec = pl.BlockSpec((1, D), lambda i: (0, 0))
    return pl.pallas_call(
        functools.partial(_ln_body, alpha=alpha),
        grid=(N // tm,),
        in_specs=[row, row, vec, vec],
        out_specs=[row, row, half],
        out_shape=[jax.ShapeDtypeStruct((N, D), F32), jax.ShapeDtypeStruct((N, D), BF16),
                   jax.ShapeDtypeStruct((N, D // 2), jnp.uint32)],
        compiler_params=_params("parallel"),
        name="ln_residual",
    )(h, y, g.reshape(1, D), b.reshape(1, D))


def _attn_body(*refs, n_maps, dk, tq, tk, has_kpe, finalize):
    q_ref, k_ref, v_ref = refs[:3]
    pos = 3
    kpe_ref = None
    if has_kpe:
        kpe_ref, pos = refs[3], 4
    fin_refs = refs[pos:-5]
    o_ref, m_sc, l_sc, acc_sc, s_sc = refs[-5:]
    qi = pl.program_id(2)
    ratio = tq // tk

    m_sc[...] = jnp.full(m_sc.shape, -jnp.inf, F32)
    l_sc[...] = jnp.zeros(l_sc.shape, F32)
    acc_sc[...] = jnp.zeros(acc_sc.shape, F32)

    def scores(ki, m, r0):
        off = pl.multiple_of(ki * tk, tk)
        k = k_ref[pl.ds(off, tk), :]
        if has_kpe:
            k = jnp.concatenate([k, kpe_ref[pl.ds(off, tk), :]], axis=-1)
        return lax.dot_general(q_ref[r0:, m * dk:(m + 1) * dk], k[:, m * dk:(m + 1) * dk],
                               (((1,), (1,)), ((), ())), preferred_element_type=F32)

    def accumulate(ki, m, s, r0):
        v = v_ref[pl.ds(pl.multiple_of(ki * tk, tk), tk), :]
        m_old = m_sc[m, r0:]
        m_new = jnp.maximum(m_old, jnp.max(s, axis=-1, keepdims=True))
        p = jnp.exp2(s - _lane_tile(m_new, tk))
        alpha = jnp.exp2(m_old - m_new)
        l_sc[m, r0:] = alpha * l_sc[m, r0:] + jnp.sum(p, axis=-1, keepdims=True)
        acc_sc[m, r0:] = (_lane_tile(alpha, acc_sc.shape[2]) * acc_sc[m, r0:]
                          + jnp.dot(p.astype(BF16), v, preferred_element_type=F32))
        m_sc[m, r0:] = m_new

    def causal(s):
        row = lax.broadcasted_iota(jnp.int32, s.shape, 0)
        col = lax.broadcasted_iota(jnp.int32, s.shape, 1)
        return jnp.where(col <= row, s, -jnp.inf)

    for m in range(n_maps):
        s_sc[m] = scores(0, m, 0)

    def full_step(ki, carry):
        for m in range(n_maps):
            s_next = scores(ki + 1, m, 0)
            accumulate(ki, m, s_sc[m], 0)
            s_sc[m] = s_next
        return carry

    n_full = ratio * qi
    lax.fori_loop(0, n_full, full_step, 0)
    for m in range(n_maps):
        accumulate(n_full, m, causal(s_sc[m]), 0)
    for j in range(1, ratio):
        for m in range(n_maps):
            accumulate(n_full + j, m, causal(scores(n_full + j, m, j * tk)), j * tk)
    finalize(o_ref, l_sc, acc_sc, *fin_refs)


def _lane_tile(x, width):
    return jnp.tile(x, (1, width // LANES))


def _fin_plain(o_ref, l_sc, acc_sc):
    o_ref[...] = (acc_sc[0] / _lane_tile(l_sc[0], acc_sc.shape[2])).astype(o_ref.dtype)


def _fin_diff(o_ref, l_sc, acc_sc, lam_ref, subln_ref, *, lambda_init):
    lam = lam_ref[...]
    lam_full = (jnp.exp(jnp.sum(lam[0:1] * lam[1:2], axis=-1, keepdims=True))
                - jnp.exp(jnp.sum(lam[2:3] * lam[3:4], axis=-1, keepdims=True)) + lambda_init)
    dv = acc_sc.shape[2]
    o = acc_sc[0] / _lane_tile(l_sc[0], dv) - lam_full * (acc_sc[1] / _lane_tile(l_sc[1], dv))
    ms = jnp.mean(o * o, axis=-1, keepdims=True)
    o = o * lax.rsqrt(ms + LN_EPS) * subln_ref[...] * (1.0 - lambda_init)
    o_ref[...] = o.astype(o_ref.dtype)


def _attention(q, k, v, *, B, S, H, n_maps, dk, dv, q_col0, k_col0, v_col0, k_width,
               kpe=None, finalize, fin_args=(), fin_specs=(), name):
    tk = _tile(S, ATTN_KEY_TILE)
    T = _tile(S, ATTN_QUERY_TILE)
    assert T % tk == 0
    nq = S // T
    in_specs = [pl.BlockSpec((T, n_maps * dk), lambda b, h, i: (b * nq + i, q_col0 + h)),
                pl.BlockSpec((S, k_width), lambda b, h, i: (b, k_col0 + h)),
                pl.BlockSpec((S, dv), lambda b, h, i: (b, v_col0 + h))]
    args = [q, k, v]
    if kpe is not None:
        in_specs.append(pl.BlockSpec((S, kpe.shape[1]), lambda b, h, i: (b, 0)))
        args.append(kpe)
    return pl.pallas_call(
        functools.partial(_attn_body, n_maps=n_maps, dk=dk, tq=T, tk=tk, has_kpe=kpe is not None,
                          finalize=finalize),
        grid=(B, H, nq),
        in_specs=in_specs + list(fin_specs),
        out_specs=pl.BlockSpec((T, dv), lambda b, h, i: (b * nq + i, h)),
        out_shape=jax.ShapeDtypeStruct((B * S, H * dv), BF16),
        scratch_shapes=[pltpu.VMEM((n_maps, T, LANES), F32), pltpu.VMEM((n_maps, T, LANES), F32),
                        pltpu.VMEM((n_maps, T, dv), F32), pltpu.VMEM((n_maps, T, tk), F32)],
        compiler_params=_params("parallel", "parallel", "parallel"),
        name=name,
    )(*args, *fin_args)


def _shared_body(x_ref, wg_ref, wu_ref, wd_ref, o_ref):
    x = x_ref[...]
    g = jnp.dot(x, wg_ref[...], preferred_element_type=F32)
    u = jnp.dot(x, wu_ref[...], preferred_element_type=F32)
    a = (g * jax.nn.sigmoid(g) * u).astype(BF16)
    o_ref[...] = jnp.dot(a, wd_ref[...], preferred_element_type=F32).astype(o_ref.dtype)


def _shared_expert(xb, wg, wu, wd):
    N, D = xb.shape
    F = wg.shape[1]
    tm = _tile(N, 512)
    full = lambda shape: pl.BlockSpec(shape, lambda i: (0, 0))
    return pl.pallas_call(
        _shared_body,
        grid=(N // tm,),
        in_specs=[pl.BlockSpec((tm, D), lambda i: (i, 0)), full((D, F)), full((D, F)), full((F, D))],
        out_specs=pl.BlockSpec((tm, D), lambda i: (i, 0)),
        out_shape=jax.ShapeDtypeStruct((N, D), BF16),
        compiler_params=_params("parallel"),
        name="moe_shared",
    )(xb, wg, wu, wd)


def _expert_body(be_ref, nu_ref, tok_ref, tokn_ref, x_hbm, wgu_ref, wd_ref, y_ref,
                 xbuf, sem, wg_sc, wu_sc, wd_sc, *, rows, ffn, k_chunk, n_chunk):
    b = pl.program_id(0)
    last = pl.num_programs(0) - 1
    nu = nu_ref[0]
    slot = b % 2

    def row_copy(idx_ref, r, s):
        return pltpu.make_async_copy(x_hbm.at[pl.ds(idx_ref[0, 0, r], 1), :],
                                     xbuf.at[s, pl.ds(r, 1), :], sem.at[s])

    def wait_gather(s):
        pltpu.make_async_copy(x_hbm.at[pl.ds(0, rows), :], xbuf.at[s], sem.at[s]).wait()

    @pl.when(b == 0)
    def _():
        def body(r, carry):
            row_copy(tok_ref, r, 0).start()
            return carry
        lax.fori_loop(0, rows, body, 0, unroll=8)

    @pl.when(b <= nu)
    def _():
        wait_gather(slot)

    @pl.when(b < nu)
    def _():
        @pl.when((b == 0) | (be_ref[b] != be_ref[jnp.maximum(b - 1, 0)]))
        def _():
            wg_sc[...] = wgu_ref[:, :ffn].astype(BF16)
            wu_sc[...] = wgu_ref[:, ffn:].astype(BF16)
            wd_sc[...] = wd_ref[...].astype(BF16)

        half = xbuf.shape[2]
        n_chunks = half // k_chunk
        n_down = half // n_chunk
        work = [2 * k_chunk * 2 * ffn] * n_chunks + [2 * n_chunk * ffn] * n_down
        edges = [rows * sum(work[:i]) // sum(work) for i in range(len(work) + 1)]

        def start_next_rows(phase):
            for i in range(edges[phase], edges[phase + 1]):
                row_copy(tokn_ref, (i * ROW_ISSUE_STRIDE) % rows, 1 - slot).start(priority=1)

        g = u = None
        for c in range(n_chunks):
            start_next_rows(c)
            lo, hi = _unpack_halves(xbuf[slot, :, c * k_chunk:(c + 1) * k_chunk])
            for part, k0 in ((lo, c * k_chunk), (hi, half + c * k_chunk)):
                xc = part.astype(BF16)
                dg = jnp.dot(xc, wg_sc[k0:k0 + k_chunk, :], preferred_element_type=F32)
                du = jnp.dot(xc, wu_sc[k0:k0 + k_chunk, :], preferred_element_type=F32)
                g = dg if g is None else g + dg
                u = du if u is None else u + du
        a = (g * jax.nn.sigmoid(g) * u).astype(BF16)
        for c in range(n_down):
            start_next_rows(n_chunks + c)
            n0 = c * n_chunk
            y_lo = jnp.dot(a, wd_sc[:, n0:n0 + n_chunk], preferred_element_type=F32)
            y_hi = jnp.dot(a, wd_sc[:, half + n0:half + n0 + n_chunk], preferred_element_type=F32)
            y_ref[:, n0:n0 + n_chunk] = _pack_halves(jnp.concatenate([y_lo, y_hi], axis=1))

    @pl.when(b >= nu)
    def _():
        y_ref[...] = jnp.zeros(y_ref.shape, y_ref.dtype)

    @pl.when((b == last) & (b < nu))
    def _():
        wait_gather(1 - slot)


def _routed_experts(hp, row_tok, block_e, n_used, w_gate_up, w_down, layer):
    N, half = hp.shape
    D = 2 * half
    _, E, _, F2 = w_gate_up.shape
    F = F2 // 2
    C = EXPERT_ROWS
    n_blocks = block_e.shape[0]
    k_chunk = _tile(half, 256)
    n_chunk = _tile(half, 512)
    assert math.gcd(ROW_ISSUE_STRIDE, C) == 1
    tok3 = row_tok.reshape(n_blocks, 1, C)
    smem_blk = lambda fn: pl.BlockSpec((1, 1, C), fn, memory_space=pltpu.SMEM)
    grid_spec = pltpu.PrefetchScalarGridSpec(
        num_scalar_prefetch=2,
        grid=(n_blocks,),
        in_specs=[smem_blk(lambda b, be, nu: (b, 0, 0)),
                  smem_blk(lambda b, be, nu: (jnp.minimum(b + 1, n_blocks - 1), 0, 0)),
                  pl.BlockSpec(memory_space=pl.ANY),
                  pl.BlockSpec((None, None, D, F2), lambda b, be, nu: (layer, be[b], 0, 0)),
                  pl.BlockSpec((None, None, F, D), lambda b, be, nu: (layer, be[b], 0, 0))],
        out_specs=pl.BlockSpec((C, half), lambda b, be, nu: (b, 0)),
        scratch_shapes=[pltpu.VMEM((2, C, half), jnp.uint32), pltpu.SemaphoreType.DMA((2,)),
                        pltpu.VMEM((D, F), BF16), pltpu.VMEM((D, F), BF16), pltpu.VMEM((F, D), BF16)],
    )
    return pl.pallas_call(
        functools.partial(_expert_body, rows=C, ffn=F, k_chunk=k_chunk, n_chunk=n_chunk),
        grid_spec=grid_spec,
        out_shape=jax.ShapeDtypeStruct((n_blocks * C, half), jnp.uint32),
        compiler_params=_params("arbitrary"),
        name="moe_experts",
    )(block_e, n_used, tok3, tok3, hp, w_gate_up, w_down)


def _combine_body(pos_ref, posn_ref, y_hbm, gate_ref, h_ref, sh_ref, g_ref, b_ref, of_ref, ob_ref,
                  ybuf, sem, *, tokens, top_k, alpha):
    i = pl.program_id(0)
    last = pl.num_programs(0) - 1
    slot = i % 2

    def row_copy(idx_ref, t, k, s):
        return pltpu.make_async_copy(y_hbm.at[pl.ds(idx_ref[0, 0, t * top_k + k], 1), :],
                                     ybuf.at[s, k, pl.ds(t, 1), :], sem.at[s])

    def wait_gather(s):
        for k in range(top_k):
            pltpu.make_async_copy(y_hbm.at[pl.ds(0, tokens), :], ybuf.at[s, k], sem.at[s]).wait()

    @pl.when(i == 0)
    def _():
        def body(t, carry):
            for k in range(top_k):
                row_copy(pos_ref, t, k, 0).start()
            return carry
        lax.fori_loop(0, tokens, body, 0)

    wait_gather(slot)

    r_lo = r_hi = None
    for k in range(top_k):
        for t in range(tokens):
            row_copy(posn_ref, t, k, 1 - slot).start(priority=t % 2)
        lo, hi = _unpack_halves(ybuf[slot, k])
        gate = gate_ref[k]
        r_lo = gate * lo if r_lo is None else r_lo + gate * lo
        r_hi = gate * hi if r_hi is None else r_hi + gate * hi

    @pl.when(i == last)
    def _():
        wait_gather(1 - slot)

    routed = jnp.concatenate([r_lo, r_hi], axis=1)
    z = alpha * h_ref[...] + (routed + sh_ref[...].astype(F32))
    mu = jnp.mean(z, axis=-1, keepdims=True)
    zc = z - mu
    var = jnp.mean(zc * zc, axis=-1, keepdims=True)
    o = zc * lax.rsqrt(var + LN_EPS) * g_ref[...] + b_ref[...]
    of_ref[...] = o
    ob_ref[...] = o.astype(BF16)


def _combine_ln(y_sorted, dest, gates, h32, shared, g, b, alpha):
    N, D = h32.shape
    K = dest.shape[0]
    T = COMBINE_TOKENS
    assert N % T == 0
    n_tiles = N // T
    pos3 = dest.T.reshape(n_tiles, 1, T * K)
    smem_blk = lambda fn: pl.BlockSpec((1, 1, T * K), fn, memory_space=pltpu.SMEM)
    row = pl.BlockSpec((T, D), lambda i: (i, 0))
    vec = pl.BlockSpec((1, D), lambda i: (0, 0))
    return pl.pallas_call(
        functools.partial(_combine_body, tokens=T, top_k=K, alpha=alpha),
        grid=(n_tiles,),
        in_specs=[smem_blk(lambda i: (i, 0, 0)),
                  smem_blk(lambda i: (jnp.minimum(i + 1, n_tiles - 1), 0, 0)),
                  pl.BlockSpec(memory_space=pl.ANY),
                  pl.BlockSpec((K, T, 1), lambda i: (0, i, 0)), row, row, vec, vec],
        out_specs=[row, row],
        out_shape=[jax.ShapeDtypeStruct((N, D), F32), jax.ShapeDtypeStruct((N, D), BF16)],
        scratch_shapes=[pltpu.VMEM((2, K, T, D // 2), jnp.uint32), pltpu.SemaphoreType.DMA((2,))],
        compiler_params=_params("arbitrary"),
        name="moe_combine_ln",
    )(pos3, pos3, y_sorted, gates.reshape(K, N, 1), h32, shared, g.reshape(1, D), b.reshape(1, D))


def _router_body(x_ref, w_ref, bias_ref, ek_ref, rk_ref, gk_ref, cnt_ref, carry_sc, tri_sc,
                 *, n_groups, topk_groups, top_k, scale):
    i = pl.program_id(0)
    E, T = w_ref.shape[0], x_ref.shape[0]
    G, Eg = n_groups, w_ref.shape[0] // n_groups
    neg = -jnp.inf

    @pl.when(i == 0)
    def _():
        carry_sc[...] = jnp.zeros(carry_sc.shape, F32)
        r = lax.broadcasted_iota(jnp.int32, (T, T), 0)
        c = lax.broadcasted_iota(jnp.int32, (T, T), 1)
        tri_sc[...] = jnp.where(r < c, 1.0, 0.0).astype(BF16)

    logits = lax.dot_general(w_ref[...], x_ref[...], (((1,), (1,)), ((), ())), preferred_element_type=F32)
    scores = jax.nn.sigmoid(logits)
    choice = scores + bias_ref[...]
    sub = lax.broadcasted_iota(jnp.int32, (Eg, T), 0)
    sc_g = [scores[g * Eg:(g + 1) * Eg] for g in range(G)]
    ch_g = [choice[g * Eg:(g + 1) * Eg] for g in range(G)]

    def first_index(hit, idx, size):
        return jnp.min(jnp.where(hit, idx, size), axis=0, keepdims=True)

    gs = []
    for x in ch_g:
        m1 = jnp.max(x, axis=0, keepdims=True)
        first = first_index(x == m1, sub, Eg)
        m2 = jnp.max(jnp.where(sub == first, neg, x), axis=0, keepdims=True)
        gs.append(m1 + m2)
    gs = jnp.concatenate(gs, axis=0)
    gid = lax.broadcasted_iota(jnp.int32, (G, T), 0)
    g_on = jnp.zeros((G, T), F32)
    for _ in range(topk_groups):
        m = jnp.max(gs, axis=0, keepdims=True)
        pick = gid == first_index(gs == m, gid, G)
        g_on = jnp.where(pick, 1.0, g_on)
        gs = jnp.where(pick, neg, gs)
    mk = [jnp.where(g_on[g:g + 1] > 0.0, ch_g[g], neg) for g in range(G)]

    eid = [sub + g * Eg for g in range(G)]
    sel = [jnp.zeros((Eg, T), F32) for _ in range(G)]
    e_k, s_k = [], []
    for _ in range(top_k):
        m = mk[0]
        for g in range(1, G):
            m = jnp.maximum(m, mk[g])
        m = jnp.max(m, axis=0, keepdims=True)
        cand = jnp.where(mk[0] == m, eid[0], E)
        for g in range(1, G):
            cand = jnp.minimum(cand, jnp.where(mk[g] == m, eid[g], E))
        first = jnp.min(cand, axis=0, keepdims=True)
        s = jnp.zeros((Eg, T), F32)
        for g in range(G):
            pick = eid[g] == first
            s = s + jnp.where(pick, sc_g[g], 0.0)
            mk[g] = jnp.where(pick, neg, mk[g])
            sel[g] = jnp.where(pick, 1.0, sel[g])
        e_k.append(first)
        s_k.append(jnp.sum(s, axis=0, keepdims=True))
    denom = s_k[0]
    for s in s_k[1:]:
        denom = denom + s
    gates = [s / denom * scale for s in s_k]

    sel_b = jnp.concatenate(sel, axis=0).astype(BF16)
    rank = (jnp.dot(sel_b, tri_sc[...], preferred_element_type=F32)
            + _lane_tile(carry_sc[...], T))
    total = carry_sc[...] + jnp.dot(sel_b, jnp.ones((T, LANES), BF16), preferred_element_type=F32)
    carry_sc[...] = total
    cnt_ref[...] = total
    r_k = []
    for first in e_k:
        r = jnp.zeros((Eg, T), F32)
        for g in range(G):
            r = r + jnp.where(eid[g] == first, rank[g * Eg:(g + 1) * Eg], 0.0)
        r_k.append(jnp.sum(r, axis=0, keepdims=True))
    ek_ref[...] = jnp.concatenate(e_k, axis=0)
    rk_ref[...] = jnp.concatenate(r_k, axis=0).astype(jnp.int32)
    gk_ref[...] = jnp.concatenate(gates, axis=0)


def _router(hb, w_router, bias):
    N, D = hb.shape
    E = w_router.shape[1]
    T = _tile(N, 512)
    assert E // N_GROUPS == 8 and T % LANES == 0
    kn = pl.BlockSpec((TOP_K, T), lambda i: (0, i))
    ek, rk, gk, cnt = pl.pallas_call(
        functools.partial(_router_body, n_groups=N_GROUPS, topk_groups=TOPK_GROUPS, top_k=TOP_K,
                          scale=ROUTED_SCALE),
        grid=(N // T,),
        in_specs=[pl.BlockSpec((T, D), lambda i: (i, 0)),
                  pl.BlockSpec((E, D), lambda i: (0, 0)),
                  pl.BlockSpec((E, 1), lambda i: (0, 0))],
        out_specs=[kn, kn, kn, pl.BlockSpec((E, LANES), lambda i: (0, 0))],
        out_shape=[jax.ShapeDtypeStruct((TOP_K, N), jnp.int32), jax.ShapeDtypeStruct((TOP_K, N), jnp.int32),
                   jax.ShapeDtypeStruct((TOP_K, N), F32), jax.ShapeDtypeStruct((E, LANES), F32)],
        scratch_shapes=[pltpu.VMEM((E, LANES), F32), pltpu.VMEM((T, T), BF16)],
        compiler_params=_params("arbitrary"),
        name="moe_router",
    )(hb, w_router.T.astype(BF16), bias.astype(F32).reshape(E, 1))
    return ek, rk, gk, cnt[:, 0].astype(jnp.int32)


def _row_layout(ek, rk, counts):
    K, N = ek.shape
    E = counts.shape[0]
    C = EXPERT_ROWS
    n_blocks = -(-(N * K) // C) + E
    padded = (counts + C - 1) // C * C
    pad_end = jnp.cumsum(padded)
    expert = jnp.arange(E, dtype=jnp.int32)[:, None, None]
    dest = jnp.sum(jnp.where(ek[None] == expert, (pad_end - padded)[:, None, None], 0), axis=0) + rk
    token = jnp.broadcast_to(jnp.arange(N, dtype=jnp.int32)[None, :], (K, N))
    row_tok = jnp.zeros((n_blocks * C,), jnp.int32).at[dest.reshape(-1)].set(
        token.reshape(-1), unique_indices=True)
    block_start = jnp.arange(n_blocks, dtype=jnp.int32) * C
    block_e = jnp.minimum(jnp.sum(pad_end[None, :] <= block_start[:, None], axis=1), E - 1).astype(jnp.int32)
    n_used = (pad_end[-1:] // C).astype(jnp.int32)
    return dest, row_tok, block_e, n_used


def _moe_ln(h32, hb, hp, layer, w_router, router_bias, w_gate_up, w_down, sh_w_gate_up, sh_w_down, g, b, alpha):
    F = sh_w_gate_up.shape[1] // 2
    ek, rk, gates, counts = _router(hb, w_router, router_bias)
    dest, row_tok, block_e, n_used = _row_layout(ek, rk, counts)
    y_sorted = _routed_experts(hp, row_tok, block_e, n_used, w_gate_up, w_down, layer)
    shared = _shared_expert(hb, sh_w_gate_up[:, :F].astype(BF16), sh_w_gate_up[:, F:].astype(BF16),
                            sh_w_down.astype(BF16))
    return _combine_ln(y_sorted, dest, gates, h32, shared, g, b, alpha)


def _diff_attention(hb, w_qkv, lam, subln, w_o, lambda_init, B, S):
    N, D = hb.shape
    d = lam.shape[1]
    H = D // (2 * d)
    tn = _tile(D, 1024)
    tm = _tile(S, 1024)
    tabs = _rope_tables_full(S, d, d ** -0.5 * LOG2E)
    n_rope_tiles = 2 * D // tn
    nsb = S // tm
    qkv = _matmul(
        hb, w_qkv.astype(BF16), out_dtype=BF16, tm=tm, tn=tn,
        epilogue=functools.partial(_ep_qkv, n_rope_tiles=n_rope_tiles),
        extras=(tabs,),
        extra_specs=(pl.BlockSpec((None, 2, tm, d),
                                  lambda j, i: (jnp.minimum(j // (n_rope_tiles // 2), 1), 0, i % nsb, 0)),),
        name="diff_qkv")
    full = lambda shape: pl.BlockSpec(shape, lambda b, h, i: (0, 0))
    o = _attention(qkv, qkv, qkv, B=B, S=S, H=H, n_maps=2, dk=d, dv=2 * d,
                   q_col0=0, k_col0=H, v_col0=2 * H, k_width=2 * d,
                   finalize=functools.partial(_fin_diff, lambda_init=lambda_init),
                   fin_args=(lam, subln.reshape(1, 2 * d)),
                   fin_specs=(full((4, d)), full((1, 2 * d))), name="diff_attn")
    return _matmul(o, w_o.astype(BF16), out_dtype=BF16, epilogue=_ep_plain, tm=1024, name="diff_out")


def _mla_shared_kv(hb, w_a, kv_norm, w_b, H, vdim, S):
    N, D = hb.shape
    R = kv_norm.shape[0]
    rope = w_a.shape[1] - R
    hw = w_b.shape[1] // H
    nope = hw - vdim
    assert nope == vdim == LANES and rope <= LANES // 2
    tm = _tile(S, 512)
    nsb = S // tm
    c_kv = _matmul(hb, w_a[:, :R].astype(BF16), out_dtype=BF16, tm=tm, tn=R,
                   epilogue=functools.partial(_ep_rmsnorm, eps=RMS_EPS),
                   extras=(kv_norm.reshape(1, R),),
                   extra_specs=(pl.BlockSpec((1, R), lambda j, i: (0, 0)),), name="mla_kv_a")
    w_pe = jnp.pad(w_a[:, R:], ((0, 0), (0, LANES - rope))).astype(BF16)
    tabs = _rope_tables_half(S, rope, 1.0)
    k_pe = _matmul(hb, w_pe, out_dtype=BF16, tm=tm, tn=LANES, epilogue=_ep_kpe, extras=(tabs,),
                   extra_specs=(pl.BlockSpec((None, 3, tm, LANES), lambda j, i: (1, 0, i % nsb, 0)),),
                   name="mla_k_pe")
    w_b3 = w_b.reshape(R, H, hw)
    w_b2 = jnp.concatenate([w_b3[:, :, :nope].reshape(R, H * nope),
                            w_b3[:, :, nope:].reshape(R, H * (hw - nope))], axis=1).astype(BF16)
    kv = _matmul(c_kv, w_b2, out_dtype=BF16, epilogue=_ep_plain, name="mla_kv_b")
    return kv, k_pe


def _mla_attention(hb, kv, k_pe, w_dq, q_norm, w_uq, w_o, H, B, S):
    N, D = hb.shape
    Qr = q_norm.shape[0]
    hq = w_uq.shape[1] // H
    nope = kv.shape[1] // (2 * H)
    rope = hq - nope
    tm = _tile(S, 512)
    nsb = S // tm
    c_q = _matmul(hb, w_dq.astype(BF16), out_dtype=BF16, tm=tm, tn=Qr,
                  epilogue=functools.partial(_ep_rmsnorm, eps=RMS_EPS),
                  extras=(q_norm.reshape(1, Qr),),
                  extra_specs=(pl.BlockSpec((1, Qr), lambda j, i: (0, 0)),), name="mla_dq")
    scale = float(hq) ** -0.5 * LOG2E
    w_q = jnp.pad(w_uq.reshape(Qr, H, hq), ((0, 0), (0, 0), (0, 2 * LANES - hq)))
    w_q = w_q.reshape(Qr, H * 2 * LANES).astype(BF16)
    tabs = _rope_tables_half(S, rope, scale)
    q = _matmul(c_q, w_q, out_dtype=BF16, tm=tm,
                epilogue=functools.partial(_ep_mla_q, scale=scale), extras=(tabs,),
                extra_specs=(pl.BlockSpec((None, 3, tm, LANES), lambda j, i: (0, 0, i % nsb, 0)),),
                name="mla_uq")
    o = _attention(q, kv, kv, B=B, S=S, H=H, n_maps=1, dk=2 * LANES, dv=nope,
                   q_col0=0, k_col0=0, v_col0=H, k_width=nope, kpe=k_pe,
                   finalize=_fin_plain, name="mla_attn")
    return _matmul(o, w_o.astype(BF16), out_dtype=BF16, epilogue=_ep_plain, tm=1024, name="mla_out")


def kernel(x, ln_g, ln_b, a_w_qkv, a_lambda, a_subln, a_w_o, kv_w_a, kv_norm, kv_w_b, b_w_dq, b_q_norm, b_w_uq, b_w_o, moe_w_router, moe_router_bias, moe_w_gate_up, moe_w_down, moe_sh_w_gate_up, moe_sh_w_down):
    B, S, D = x.shape
    depth = ln_g.shape[0]
    n_a = a_w_qkv.shape[0]
    alpha = (2 * depth) ** 0.25
    rope_dim = kv_w_a.shape[1] - kv_norm.shape[0]
    mla_heads = (b_w_uq.shape[2] - kv_w_b.shape[1] + b_w_o.shape[1]) // rope_dim
    h32 = x.reshape(B * S, D)
    hb = h32.astype(BF16)
    kv = k_pe = None
    for l in range(depth):
        if l < n_a:
            lambda_init = 0.8 - 0.6 * math.exp(-0.3 * l)
            y = _diff_attention(hb, a_w_qkv[l], a_lambda[l], a_subln[l], a_w_o[l], lambda_init, B, S)
        else:
            if l == n_a:
                kv, k_pe = _mla_shared_kv(hb, kv_w_a, kv_norm, kv_w_b, mla_heads,
                                          b_w_o.shape[1] // mla_heads, S)
            j = l - n_a
            y = _mla_attention(hb, kv, k_pe, b_w_dq[j], b_q_norm[j], b_w_uq[j], b_w_o[j], mla_heads, B, S)
        h32, hb, hp = _ln_residual(h32, y, ln_g[l, 0], ln_b[l, 0], alpha)
        h32, hb = _moe_ln(h32, hb, hp, l, moe_w_router[l], moe_router_bias[l], moe_w_gate_up, moe_w_down,
                          moe_sh_w_gate_up[l], moe_sh_w_down[l], ln_g[l, 1], ln_b[l, 1], alpha)
    return h32.reshape(B, S, D)
```

```python
import functools
import math

import jax
import jax.numpy as jnp
from jax import lax
from jax.experimental import pallas as pl
from jax.experimental.pallas import tpu as pltpu

F32 = jnp.float32
BF16 = jnp.bfloat16

N_GROUPS = 8
TOPK_GROUPS = 4
TOP_K = 8
ROUTED_SCALE = 2.5
ROPE_THETA = 10000.0
LN_EPS = 1e-5
RMS_EPS = 1e-6
LOG2E = math.log2(math.e)

LANES = 128
V7X_VMEM_LIMIT = 56 * 1024 * 1024
EXPERT_ROWS = 512
ROW_ISSUE_STRIDE = 67
COMBINE_TOKENS = 64
ATTN_QUERY_TILE = 1024
ATTN_KEY_TILE = 512
ATTN_ROW_BLOCK = 512


def _params(*sem):
    return pltpu.CompilerParams(dimension_semantics=sem, vmem_limit_bytes=V7X_VMEM_LIMIT)


def _tile(n, pref):
    if n <= pref:
        return n
    t = pref - pref % LANES
    while t >= LANES:
        if n % t == 0:
            return t
        t -= LANES
    return n


def _mm_body(x_ref, w_ref, *rest, epilogue, n_extra):
    extra, o_ref = rest[:n_extra], rest[n_extra]
    acc = jnp.dot(x_ref[...], w_ref[...], preferred_element_type=F32)
    epilogue(acc, o_ref, *extra)


def _matmul(x, w, *, out_dtype, epilogue, extras=(), extra_specs=(), tm=512, tn=1024, name):
    M, K = x.shape
    N = w.shape[1]
    tm, tn = _tile(M, tm), _tile(N, tn)
    return pl.pallas_call(
        functools.partial(_mm_body, epilogue=epilogue, n_extra=len(extras)),
        grid=(N // tn, M // tm),
        in_specs=[pl.BlockSpec((tm, K), lambda j, i: (i, 0)),
                  pl.BlockSpec((K, tn), lambda j, i: (0, j)),
                  *extra_specs],
        out_specs=pl.BlockSpec((tm, tn), lambda j, i: (i, j)),
        out_shape=jax.ShapeDtypeStruct((M, N), out_dtype),
        compiler_params=_params("parallel", "parallel"),
        name=name,
    )(x, w, *extras)


def _ep_plain(acc, o_ref):
    o_ref[...] = acc.astype(o_ref.dtype)


def _ep_rmsnorm(acc, o_ref, g_ref, *, eps):
    ms = jnp.mean(acc * acc, axis=-1, keepdims=True)
    o_ref[...] = (acc * lax.rsqrt(ms + eps) * g_ref[...]).astype(o_ref.dtype)


def _rope_full_group(seg, cos, sin):
    return seg * cos + pltpu.roll(seg, LANES // 2, 1) * sin


def _rope_half_group(seg, c, sa, sb):
    return seg * c + pltpu.roll(seg, 3 * LANES // 4, 1) * sa + pltpu.roll(seg, LANES // 4, 1) * sb


def _ep_qkv(acc, o_ref, tab_ref, *, n_rope_tiles):
    j = pl.program_id(0)

    @pl.when(j < n_rope_tiles)
    def _():
        cos, sin = tab_ref[0], tab_ref[1]
        for g in range(acc.shape[1] // LANES):
            sl = slice(g * LANES, (g + 1) * LANES)
            o_ref[:, sl] = _rope_full_group(acc[:, sl], cos, sin).astype(o_ref.dtype)

    @pl.when(j >= n_rope_tiles)
    def _():
        o_ref[...] = acc.astype(o_ref.dtype)


def _ep_mla_q(acc, o_ref, tab_ref, *, scale):
    c, sa, sb = tab_ref[0], tab_ref[1], tab_ref[2]
    for g in range(acc.shape[1] // LANES):
        sl = slice(g * LANES, (g + 1) * LANES)
        seg = acc[:, sl]
        out = seg * scale if g % 2 == 0 else _rope_half_group(seg, c, sa, sb)
        o_ref[:, sl] = out.astype(o_ref.dtype)


def _ep_kpe(acc, o_ref, tab_ref):
    o_ref[...] = _rope_half_group(acc, tab_ref[0], tab_ref[1], tab_ref[2]).astype(o_ref.dtype)


def _rope_tables_full(S, d, scale):
    inv = ROPE_THETA ** (-jnp.arange(0, d, 2, dtype=F32) / d)
    ang = jnp.arange(S, dtype=F32)[:, None] * inv[None, :]
    cos = jnp.concatenate([jnp.cos(ang), jnp.cos(ang)], axis=-1)
    sin = jnp.concatenate([-jnp.sin(ang), jnp.sin(ang)], axis=-1)
    t = jnp.stack([cos, sin])
    return jnp.stack([t * scale, t])


def _rope_tables_half(S, d, scale):
    inv = ROPE_THETA ** (-jnp.arange(0, d, 2, dtype=F32) / d)
    ang = jnp.arange(S, dtype=F32)[:, None] * inv[None, :]
    cos, sin = jnp.cos(ang), jnp.sin(ang)
    z = jnp.zeros_like(cos)
    pad = jnp.zeros((S, LANES - d), F32)
    c = jnp.concatenate([cos, cos, pad], axis=-1)
    sa = jnp.concatenate([-sin, z, pad], axis=-1)
    sb = jnp.concatenate([z, sin, pad], axis=-1)
    t = jnp.stack([c, sa, sb])
    return jnp.stack([t * scale, t])


def _pack_halves(x):
    half = x.shape[1] // 2
    lo = lax.bitcast_convert_type(x[:, :half].astype(BF16).astype(F32), jnp.uint32)
    hi = lax.bitcast_convert_type(x[:, half:].astype(BF16).astype(F32), jnp.uint32)
    return (lo >> 16) | (hi & jnp.uint32(0xFFFF0000))


def _unpack_halves(w):
    lo = lax.bitcast_convert_type(w << 16, F32)
    hi = lax.bitcast_convert_type(w & jnp.uint32(0xFFFF0000), F32)
    return lo, hi


def _ln_body(h_ref, y_ref, g_ref, b_ref, of_ref, ob_ref, op_ref, *, alpha):
    z = alpha * h_ref[...] + y_ref[...].astype(F32)
    mu = jnp.mean(z, axis=-1, keepdims=True)
    zc = z - mu
    var = jnp.mean(zc * zc, axis=-1, keepdims=True)
    o = zc * lax.rsqrt(var + LN_EPS) * g_ref[...] + b_ref[...]
    of_ref[...] = o
    ob_ref[...] = o.astype(BF16)
    op_ref[...] = _pack_halves(o)


def _ln_residual(h, y, g, b, alpha):
    N, D = h.shape
    tm = _tile(N, 256)
    row = pl.BlockSpec((tm, D), lambda i: (i, 0))
    half = pl.BlockSpec((tm, D // 2), lambda i: (i, 0))
    vec = pl.BlockSpec((1, D), lambda i: (0, 0))
    return pl.pallas_call(
        functools.partial(_ln_body, alpha=alpha),
        grid=(N // tm,),
        in_specs=[row, row, vec, vec],
        out_specs=[row, row, half],
        out_shape=[jax.ShapeDtypeStruct((N, D), F32), jax.ShapeDtypeStruct((N, D), BF16),
                   jax.ShapeDtypeStruct((N, D // 2), jnp.uint32)],
        compiler_params=_params("parallel"),
        name="ln_residual",
    )(h, y, g.reshape(1, D), b.reshape(1, D))


def _attn_body(*refs, n_maps, dk, tq, tk, has_kpe, finalize):
    q_ref, k_ref, v_ref = refs[:3]
    pos = 3
    kpe_ref = None
    if has_kpe:
        kpe_ref, pos = refs[3], 4
    fin_refs = refs[pos:-5]
    o_ref, m_sc, l_sc, acc_sc, s_sc = refs[-5:]
    qi = pl.program_id(2)
    ratio = tq // tk

    m_sc[...] = jnp.full(m_sc.shape, -jnp.inf, F32)
    l_sc[...] = jnp.zeros(l_sc.shape, F32)
    acc_sc[...] = jnp.zeros(acc_sc.shape, F32)

    def scores(ki, m, r0):
        off = pl.multiple_of(ki * tk, tk)
        k = k_ref[pl.ds(off, tk), :]
        if has_kpe:
            k = jnp.concatenate([k, kpe_ref[pl.ds(off, tk), :]], axis=-1)
        return lax.dot_general(q_ref[r0:, m * dk:(m + 1) * dk], k[:, m * dk:(m + 1) * dk],
                               (((1,), (1,)), ((), ())), preferred_element_type=F32)

    def accumulate(ki, m, s, r0):
        v = v_ref[pl.ds(pl.multiple_of(ki * tk, tk), tk), :]
        rb = min(ATTN_ROW_BLOCK, tk)
        for h0 in range(0, s.shape[0], rb):
            rows = slice(r0 + h0, r0 + h0 + rb)
            sh = s[h0:h0 + rb]
            m_old = m_sc[m, rows]
            m_new = jnp.maximum(m_old, jnp.max(sh, axis=-1, keepdims=True))
            p = jnp.exp2(sh - _lane_tile(m_new, tk))
            alpha = jnp.exp2(m_old - m_new)
            l_sc[m, rows] = alpha * l_sc[m, rows] + jnp.sum(p, axis=-1, keepdims=True)
            acc_sc[m, rows] = (_lane_tile(alpha, acc_sc.shape[2]) * acc_sc[m, rows]
                               + jnp.dot(p.astype(BF16), v, preferred_element_type=F32))
            m_sc[m, rows] = m_new

    def causal(s):
        row = lax.broadcasted_iota(jnp.int32, s.shape, 0)
        col = lax.broadcasted_iota(jnp.int32, s.shape, 1)
        return jnp.where(col <= row, s, -jnp.inf)

    for m in range(n_maps):
        s_sc[m] = scores(0, m, 0)

    def full_step(ki, carry):
        for m in range(n_maps):
            s_next = scores(ki + 1, m, 0)
            accumulate(ki, m, s_sc[m], 0)
            s_sc[m] = s_next
        return carry

    n_full = ratio * qi
    lax.fori_loop(0, n_full, full_step, 0)
    for m in range(n_maps):
        accumulate(n_full, m, causal(s_sc[m]), 0)
    for j in range(1, ratio):
        for m in range(n_maps):
            accumulate(n_full + j, m, causal(scores(n_full + j, m, j * tk)), j * tk)
    finalize(o_ref, l_sc, acc_sc, *fin_refs)


def _lane_tile(x, width):
    return jnp.tile(x, (1, width // LANES))


def _fin_plain(o_ref, l_sc, acc_sc):
    o_ref[...] = (acc_sc[0] / _lane_tile(l_sc[0], acc_sc.shape[2])).astype(o_ref.dtype)


def _fin_diff(o_ref, l_sc, acc_sc, lam_ref, subln_ref, *, lambda_init):
    lam = lam_ref[...]
    lam_full = (jnp.exp(jnp.sum(lam[0:1] * lam[1:2], axis=-1, keepdims=True))
                - jnp.exp(jnp.sum(lam[2:3] * lam[3:4], axis=-1, keepdims=True)) + lambda_init)
    dv = acc_sc.shape[2]
    o = acc_sc[0] / _lane_tile(l_sc[0], dv) - lam_full * (acc_sc[1] / _lane_tile(l_sc[1], dv))
    ms = jnp.mean(o * o, axis=-1, keepdims=True)
    o = o * lax.rsqrt(ms + LN_EPS) * subln_ref[...] * (1.0 - lambda_init)
    o_ref[...] = o.astype(o_ref.dtype)


def _attention(q, k, v, *, B, S, H, n_maps, dk, dv, q_col0, k_col0, v_col0, k_width,
               kpe=None, finalize, fin_args=(), fin_specs=(), name):
    tk = _tile(S, ATTN_KEY_TILE)
    T = _tile(S, ATTN_QUERY_TILE)
    assert T % tk == 0
    nq = S // T
    in_specs = [pl.BlockSpec((T, n_maps * dk), lambda b, h, i: (b * nq + i, q_col0 + h)),
                pl.BlockSpec((S, k_width), lambda b, h, i: (b, k_col0 + h)),
                pl.BlockSpec((S, dv), lambda b, h, i: (b, v_col0 + h))]
    args = [q, k, v]
    if kpe is not None:
        in_specs.append(pl.BlockSpec((S, kpe.shape[1]), lambda b, h, i: (b, 0)))
        args.append(kpe)
    return pl.pallas_call(
        functools.partial(_attn_body, n_maps=n_maps, dk=dk, tq=T, tk=tk, has_kpe=kpe is not None,
                          finalize=finalize),
        grid=(B, H, nq),
        in_specs=in_specs + list(fin_specs),
        out_specs=pl.BlockSpec((T, dv), lambda b, h, i: (b * nq + i, h)),
        out_shape=jax.ShapeDtypeStruct((B * S, H * dv), BF16),
        scratch_shapes=[pltpu.VMEM((n_maps, T, LANES), F32), pltpu.VMEM((n_maps, T, LANES), F32),
                        pltpu.VMEM((n_maps, T, dv), F32), pltpu.VMEM((n_maps, T, tk), F32)],
        compiler_params=_params("parallel", "parallel", "parallel"),
        name=name,
    )(*args, *fin_args)


def _shared_body(x_ref, wg_ref, wu_ref, wd_ref, o_ref):
    x = x_ref[...]
    g = jnp.dot(x, wg_ref[...], preferred_element_type=F32)
    u = jnp.dot(x, wu_ref[...], preferred_element_type=F32)
    a = (g * jax.nn.sigmoid(g) * u).astype(BF16)
    o_ref[...] = jnp.dot(a, wd_ref[...], preferred_element_type=F32).astype(o_ref.dtype)


def _shared_expert(xb, wg, wu, wd):
    N, D = xb.shape
    F = wg.shape[1]
    tm = _tile(N, 512)
    full = lambda shape: pl.BlockSpec(shape, lambda i: (0, 0))
    return pl.pallas_call(
        _shared_body,
        grid=(N // tm,),
        in_specs=[pl.BlockSpec((tm, D), lambda i: (i, 0)), full((D, F)), full((D, F)), full((F, D))],
        out_specs=pl.BlockSpec((tm, D), lambda i: (i, 0)),
        out_shape=jax.ShapeDtypeStruct((N, D), BF16),
        compiler_params=_params("parallel"),
        name="moe_shared",
    )(xb, wg, wu, wd)


def _expert_body(be_ref, nu_ref, tok_ref, tokn_ref, x_hbm, wgu_ref, wd_ref, y_ref,
                 xbuf, sem, wg_sc, wu_sc, wd_sc, *, rows, ffn, k_chunk, n_chunk):
    b = pl.program_id(0)
    last = pl.num_programs(0) - 1
    nu = nu_ref[0]
    slot = b % 2

    def row_copy(idx_ref, r, s):
        return pltpu.make_async_copy(x_hbm.at[pl.ds(idx_ref[0, 0, r], 1), :],
                                     xbuf.at[s, pl.ds(r, 1), :], sem.at[s])

    def wait_gather(s):
        pltpu.make_async_copy(x_hbm.at[pl.ds(0, rows), :], xbuf.at[s], sem.at[s]).wait()

    @pl.when(b == 0)
    def _():
        def body(r, carry):
            row_copy(tok_ref, r, 0).start()
            return carry
        lax.fori_loop(0, rows, body, 0, unroll=8)

    @pl.when(b <= nu)
    def _():
        wait_gather(slot)

    @pl.when(b < nu)
    def _():
        @pl.when((b == 0) | (be_ref[b] != be_ref[jnp.maximum(b - 1, 0)]))
        def _():
            wg_sc[...] = wgu_ref[:, :ffn].astype(BF16)
            wu_sc[...] = wgu_ref[:, ffn:].astype(BF16)
            wd_sc[...] = wd_ref[...].astype(BF16)

        half = xbuf.shape[2]
        n_chunks = half // k_chunk
        n_down = half // n_chunk
        work = [2 * k_chunk * 2 * ffn] * n_chunks + [2 * n_chunk * ffn] * n_down
        edges = [rows * sum(work[:i]) // sum(work) for i in range(len(work) + 1)]

        def start_next_rows(phase):
            for i in range(edges[phase], edges[phase + 1]):
                row_copy(tokn_ref, (i * ROW_ISSUE_STRIDE) % rows, 1 - slot).start(priority=1)

        g = u = None
        for c in range(n_chunks):
            start_next_rows(c)
            lo, hi = _unpack_halves(xbuf[slot, :, c * k_chunk:(c + 1) * k_chunk])
            for part, k0 in ((lo, c * k_chunk), (hi, half + c * k_chunk)):
                xc = part.astype(BF16)
                dg = jnp.dot(xc, wg_sc[k0:k0 + k_chunk, :], preferred_element_type=F32)
                du = jnp.dot(xc, wu_sc[k0:k0 + k_chunk, :], preferred_element_type=F32)
                g = dg if g is None else g + dg
                u = du if u is None else u + du
        a = (g * jax.nn.sigmoid(g) * u).astype(BF16)
        for c in range(n_down):
            start_next_rows(n_chunks + c)
            n0 = c * n_chunk
            y_lo = jnp.dot(a, wd_sc[:, n0:n0 + n_chunk], preferred_element_type=F32)
            y_hi = jnp.dot(a, wd_sc[:, half + n0:half + n0 + n_chunk], preferred_element_type=F32)
            y_ref[:, n0:n0 + n_chunk] = _pack_halves(jnp.concatenate([y_lo, y_hi], axis=1))

    @pl.when(b >= nu)
    def _():
        y_ref[...] = jnp.zeros(y_ref.shape, y_ref.dtype)

    @pl.when((b == last) & (b < nu))
    def _():
        wait_gather(1 - slot)


def _routed_experts(hp, row_tok, block_e, n_used, w_gate_up, w_down, layer):
    N, half = hp.shape
    D = 2 * half
    _, E, _, F2 = w_gate_up.shape
    F = F2 // 2
    C = EXPERT_ROWS
    n_blocks = block_e.shape[0]
    k_chunk = _tile(half, 256)
    n_chunk = _tile(half, 512)
    assert math.gcd(ROW_ISSUE_STRIDE, C) == 1
    tok3 = row_tok.reshape(n_blocks, 1, C)
    smem_blk = lambda fn: pl.BlockSpec((1, 1, C), fn, memory_space=pltpu.SMEM)
    grid_spec = pltpu.PrefetchScalarGridSpec(
        num_scalar_prefetch=2,
        grid=(n_blocks,),
        in_specs=[smem_blk(lambda b, be, nu: (b, 0, 0)),
                  smem_blk(lambda b, be, nu: (jnp.minimum(b + 1, n_blocks - 1), 0, 0)),
                  pl.BlockSpec(memory_space=pl.ANY),
                  pl.BlockSpec((None, None, D, F2), lambda b, be, nu: (layer, be[b], 0, 0)),
                  pl.BlockSpec((None, None, F, D), lambda b, be, nu: (layer, be[b], 0, 0))],
        out_specs=pl.BlockSpec((C, half), lambda b, be, nu: (b, 0)),
        scratch_shapes=[pltpu.VMEM((2, C, half), jnp.uint32), pltpu.SemaphoreType.DMA((2,)),
                        pltpu.VMEM((D, F), BF16), pltpu.VMEM((D, F), BF16), pltpu.VMEM((F, D), BF16)],
    )
    return pl.pallas_call(
        functools.partial(_expert_body, rows=C, ffn=F, k_chunk=k_chunk, n_chunk=n_chunk),
        grid_spec=grid_spec,
        out_shape=jax.ShapeDtypeStruct((n_blocks * C, half), jnp.uint32),
        compiler_params=_params("arbitrary"),
        name="moe_experts",
    )(block_e, n_used, tok3, tok3, hp, w_gate_up, w_down)


def _combine_body(pos_ref, posn_ref, y_hbm, gate_ref, h_ref, sh_ref, g_ref, b_ref, of_ref, ob_ref,
                  ybuf, sem, *, tokens, top_k, alpha):
    i = pl.program_id(0)
    last = pl.num_programs(0) - 1
    slot = i % 2

    def row_copy(idx_ref, t, k, s):
        return pltpu.make_async_copy(y_hbm.at[pl.ds(idx_ref[0, 0, t * top_k + k], 1), :],
                                     ybuf.at[s, k, pl.ds(t, 1), :], sem.at[s])

    def wait_gather(s):
        for k in range(top_k):
            pltpu.make_async_copy(y_hbm.at[pl.ds(0, tokens), :], ybuf.at[s, k], sem.at[s]).wait()

    @pl.when(i == 0)
    def _():
        def body(t, carry):
            for k in range(top_k):
                row_copy(pos_ref, t, k, 0).start()
            return carry
        lax.fori_loop(0, tokens, body, 0)

    wait_gather(slot)

    r_lo = r_hi = None
    for k in range(top_k):
        for t in range(tokens):
            row_copy(posn_ref, t, k, 1 - slot).start(priority=t % 2)
        lo, hi = _unpack_halves(ybuf[slot, k])
        gate = gate_ref[k]
        r_lo = gate * lo if r_lo is None else r_lo + gate * lo
        r_hi = gate * hi if r_hi is None else r_hi + gate * hi

    @pl.when(i == last)
    def _():
        wait_gather(1 - slot)

    routed = jnp.concatenate([r_lo, r_hi], axis=1)
    z = alpha * h_ref[...] + (routed + sh_ref[...].astype(F32))
    mu = jnp.mean(z, axis=-1, keepdims=True)
    zc = z - mu
    var = jnp.mean(zc * zc, axis=-1, keepdims=True)
    o = zc * lax.rsqrt(var + LN_EPS) * g_ref[...] + b_ref[...]
    of_ref[...] = o
    ob_ref[...] = o.astype(BF16)


def _combine_ln(y_sorted, dest, gates, h32, shared, g, b, alpha):
    N, D = h32.shape
    K = dest.shape[0]
    T = COMBINE_TOKENS
    assert N % T == 0
    n_tiles = N // T
    pos3 = dest.T.reshape(n_tiles, 1, T * K)
    smem_blk = lambda fn: pl.BlockSpec((1, 1, T * K), fn, memory_space=pltpu.SMEM)
    row = pl.BlockSpec((T, D), lambda i: (i, 0))
    vec = pl.BlockSpec((1, D), lambda i: (0, 0))
    return pl.pallas_call(
        functools.partial(_combine_body, tokens=T, top_k=K, alpha=alpha),
        grid=(n_tiles,),
        in_specs=[smem_blk(lambda i: (i, 0, 0)),
                  smem_blk(lambda i: (jnp.minimum(i + 1, n_tiles - 1), 0, 0)),
                  pl.BlockSpec(memory_space=pl.ANY),
                  pl.BlockSpec((K, T, 1), lambda i: (0, i, 0)), row, row, vec, vec],
        out_specs=[row, row],
        out_shape=[jax.ShapeDtypeStruct((N, D), F32), jax.ShapeDtypeStruct((N, D), BF16)],
        scratch_shapes=[pltpu.VMEM((2, K, T, D // 2), jnp.uint32), pltpu.SemaphoreType.DMA((2,))],
        compiler_params=_params("arbitrary"),
        name="moe_combine_ln",
    )(pos3, pos3, y_sorted, gates.reshape(K, N, 1), h32, shared, g.reshape(1, D), b.reshape(1, D))


def _router_body(x_ref, w_ref, bias_ref, ek_ref, rk_ref, gk_ref, cnt_ref, carry_sc, tri_sc,
                 *, n_groups, topk_groups, top_k, scale):
    i = pl.program_id(0)
    E, T = w_ref.shape[0], x_ref.shape[0]
    G, Eg = n_groups, w_ref.shape[0] // n_groups
    neg = -jnp.inf

    @pl.when(i == 0)
    def _():
        carry_sc[...] = jnp.zeros(carry_sc.shape, F32)
        r = lax.broadcasted_iota(jnp.int32, (T, T), 0)
        c = lax.broadcasted_iota(jnp.int32, (T, T), 1)
        tri_sc[...] = jnp.where(r < c, 1.0, 0.0).astype(BF16)

    logits = lax.dot_general(w_ref[...], x_ref[...], (((1,), (1,)), ((), ())), preferred_element_type=F32)
    scores = jax.nn.sigmoid(logits)
    choice = scores + bias_ref[...]
    sub = lax.broadcasted_iota(jnp.int32, (Eg, T), 0)
    sc_g = [scores[g * Eg:(g + 1) * Eg] for g in range(G)]
    ch_g = [choice[g * Eg:(g + 1) * Eg] for g in range(G)]

    def first_index(hit, idx, size):
        return jnp.min(jnp.where(hit, idx, size), axis=0, keepdims=True)

    gs = []
    for x in ch_g:
        m1 = jnp.max(x, axis=0, keepdims=True)
        first = first_index(x == m1, sub, Eg)
        m2 = jnp.max(jnp.where(sub == first, neg, x), axis=0, keepdims=True)
        gs.append(m1 + m2)
    gs = jnp.concatenate(gs, axis=0)
    gid = lax.broadcasted_iota(jnp.int32, (G, T), 0)
    g_on = jnp.zeros((G, T), F32)
    for _ in range(topk_groups):
        m = jnp.max(gs, axis=0, keepdims=True)
        pick = gid == first_index(gs == m, gid, G)
        g_on = jnp.where(pick, 1.0, g_on)
        gs = jnp.where(pick, neg, gs)
    mk = [jnp.where(g_on[g:g + 1] > 0.0, ch_g[g], neg) for g in range(G)]

    eid = [sub + g * Eg for g in range(G)]
    sel = [jnp.zeros((Eg, T), F32) for _ in range(G)]
    e_k, s_k = [], []
    for _ in range(top_k):
        m = mk[0]
        for g in range(1, G):
            m = jnp.maximum(m, mk[g])
        m = jnp.max(m, axis=0, keepdims=True)
        cand = jnp.where(mk[0] == m, eid[0], E)
        for g in range(1, G):
            cand = jnp.minimum(cand, jnp.where(mk[g] == m, eid[g], E))
        first = jnp.min(cand, axis=0, keepdims=True)
        s = jnp.zeros((Eg, T), F32)
        for g in range(G):
            pick = eid[g] == first
            s = s + jnp.where(pick, sc_g[g], 0.0)
            mk[g] = jnp.where(pick, neg, mk[g])
            sel[g] = jnp.where(pick, 1.0, sel[g])
        e_k.append(first)
        s_k.append(jnp.sum(s, axis=0, keepdims=True))
    denom = s_k[0]
    for s in s_k[1:]:
        denom = denom + s
    gates = [s / denom * scale for s in s_k]

    sel_b = jnp.concatenate(sel, axis=0).astype(BF16)
    rank = (jnp.dot(sel_b, tri_sc[...], preferred_element_type=F32)
            + _lane_tile(carry_sc[...], T))
    total = carry_sc[...] + jnp.dot(sel_b, jnp.ones((T, LANES), BF16), preferred_element_type=F32)
    carry_sc[...] = total
    cnt_ref[...] = total
    r_k = []
    for first in e_k:
        r = jnp.zeros((Eg, T), F32)
        for g in range(G):
            r = r + jnp.where(eid[g] == first, rank[g * Eg:(g + 1) * Eg], 0.0)
        r_k.append(jnp.sum(r, axis=0, keepdims=True))
    ek_ref[...] = jnp.concatenate(e_k, axis=0)
    rk_ref[...] = jnp.concatenate(r_k, axis=0).astype(jnp.int32)
    gk_ref[...] = jnp.concatenate(gates, axis=0)


def _router(hb, w_router, bias):
    N, D = hb.shape
    E = w_router.shape[1]
    T = _tile(N, 512)
    assert E // N_GROUPS == 8 and T % LANES == 0
    kn = pl.BlockSpec((TOP_K, T), lambda i: (0, i))
    ek, rk, gk, cnt = pl.pallas_call(
        functools.partial(_router_body, n_groups=N_GROUPS, topk_groups=TOPK_GROUPS, top_k=TOP_K,
                          scale=ROUTED_SCALE),
        grid=(N // T,),
        in_specs=[pl.BlockSpec((T, D), lambda i: (i, 0)),
                  pl.BlockSpec((E, D), lambda i: (0, 0)),
                  pl.BlockSpec((E, 1), lambda i: (0, 0))],
        out_specs=[kn, kn, kn, pl.BlockSpec((E, LANES), lambda i: (0, 0))],
        out_shape=[jax.ShapeDtypeStruct((TOP_K, N), jnp.int32), jax.ShapeDtypeStruct((TOP_K, N), jnp.int32),
                   jax.ShapeDtypeStruct((TOP_K, N), F32), jax.ShapeDtypeStruct((E, LANES), F32)],
        scratch_shapes=[pltpu.VMEM((E, LANES), F32), pltpu.VMEM((T, T), BF16)],
        compiler_params=_params("arbitrary"),
        name="moe_router",
    )(hb, w_router.T.astype(BF16), bias.astype(F32).reshape(E, 1))
    return ek, rk, gk, cnt[:, 0].astype(jnp.int32)


def _row_layout(ek, rk, counts):
    K, N = ek.shape
    E = counts.shape[0]
    C = EXPERT_ROWS
    n_blocks = -(-(N * K) // C) + E
    padded = (counts + C - 1) // C * C
    pad_end = jnp.cumsum(padded)
    expert = jnp.arange(E, dtype=jnp.int32)[:, None, None]
    dest = jnp.sum(jnp.where(ek[None] == expert, (pad_end - padded)[:, None, None], 0), axis=0) + rk
    token = jnp.broadcast_to(jnp.arange(N, dtype=jnp.int32)[None, :], (K, N))
    row_tok = jnp.zeros((n_blocks * C,), jnp.int32).at[dest.reshape(-1)].set(
        token.reshape(-1), unique_indices=True)
    block_start = jnp.arange(n_blocks, dtype=jnp.int32) * C
    block_e = jnp.minimum(jnp.sum(pad_end[None, :] <= block_start[:, None], axis=1), E - 1).astype(jnp.int32)
    n_used = (pad_end[-1:] // C).astype(jnp.int32)
    return dest, row_tok, block_e, n_used


def _moe_ln(h32, hb, hp, layer, w_router, router_bias, w_gate_up, w_down, sh_w_gate_up, sh_w_down, g, b, alpha):
    F = sh_w_gate_up.shape[1] // 2
    ek, rk, gates, counts = _router(hb, w_router, router_bias)
    dest, row_tok, block_e, n_used = _row_layout(ek, rk, counts)
    y_sorted = _routed_experts(hp, row_tok, block_e, n_used, w_gate_up, w_down, layer)
    shared = _shared_expert(hb, sh_w_gate_up[:, :F].astype(BF16), sh_w_gate_up[:, F:].astype(BF16),
                            sh_w_down.astype(BF16))
    return _combine_ln(y_sorted, dest, gates, h32, shared, g, b, alpha)


def _diff_attention(hb, w_qkv, lam, subln, w_o, lambda_init, B, S):
    N, D = hb.shape
    d = lam.shape[1]
    H = D // (2 * d)
    tn = _tile(D, 1024)
    tm = _tile(S, 1024)
    tabs = _rope_tables_full(S, d, d ** -0.5 * LOG2E)
    n_rope_tiles = 2 * D // tn
    nsb = S // tm
    qkv = _matmul(
        hb, w_qkv.astype(BF16), out_dtype=BF16, tm=tm, tn=tn,
        epilogue=functools.partial(_ep_qkv, n_rope_tiles=n_rope_tiles),
        extras=(tabs,),
        extra_specs=(pl.BlockSpec((None, 2, tm, d),
                                  lambda j, i: (jnp.minimum(j // (n_rope_tiles // 2), 1), 0, i % nsb, 0)),),
        name="diff_qkv")
    full = lambda shape: pl.BlockSpec(shape, lambda b, h, i: (0, 0))
    o = _attention(qkv, qkv, qkv, B=B, S=S, H=H, n_maps=2, dk=d, dv=2 * d,
                   q_col0=0, k_col0=H, v_col0=2 * H, k_width=2 * d,
                   finalize=functools.partial(_fin_diff, lambda_init=lambda_init),
                   fin_args=(lam, subln.reshape(1, 2 * d)),
                   fin_specs=(full((4, d)), full((1, 2 * d))), name="diff_attn")
    return _matmul(o, w_o.astype(BF16), out_dtype=BF16, epilogue=_ep_plain, tm=1024, name="diff_out")


def _mla_shared_kv(hb, w_a, kv_norm, w_b, H, vdim, S):
    N, D = hb.shape
    R = kv_norm.shape[0]
    rope = w_a.shape[1] - R
    hw = w_b.shape[1] // H
    nope = hw - vdim
    assert nope == vdim == LANES and rope <= LANES // 2
    tm = _tile(S, 512)
    nsb = S // tm
    c_kv = _matmul(hb, w_a[:, :R].astype(BF16), out_dtype=BF16, tm=tm, tn=R,
                   epilogue=functools.partial(_ep_rmsnorm, eps=RMS_EPS),
                   extras=(kv_norm.reshape(1, R),),
                   extra_specs=(pl.BlockSpec((1, R), lambda j, i: (0, 0)),), name="mla_kv_a")
    w_pe = jnp.pad(w_a[:, R:], ((0, 0), (0, LANES - rope))).astype(BF16)
    tabs = _rope_tables_half(S, rope, 1.0)
    k_pe = _matmul(hb, w_pe, out_dtype=BF16, tm=tm, tn=LANES, epilogue=_ep_kpe, extras=(tabs,),
                   extra_specs=(pl.BlockSpec((None, 3, tm, LANES), lambda j, i: (1, 0, i % nsb, 0)),),
                   name="mla_k_pe")
    w_b3 = w_b.reshape(R, H, hw)
    w_b2 = jnp.concatenate([w_b3[:, :, :nope].reshape(R, H * nope),
                            w_b3[:, :, nope:].reshape(R, H * (hw - nope))], axis=1).astype(BF16)
    kv = _matmul(c_kv, w_b2, out_dtype=BF16, epilogue=_ep_plain, name="mla_kv_b")
    return kv, k_pe


def _mla_attention(hb, kv, k_pe, w_dq, q_norm, w_uq, w_o, H, B, S):
    N, D = hb.shape
    Qr = q_norm.shape[0]
    hq = w_uq.shape[1] // H
    nope = kv.shape[1] // (2 * H)
    rope = hq - nope
    tm = _tile(S, 512)
    nsb = S // tm
    c_q = _matmul(hb, w_dq.astype(BF16), out_dtype=BF16, tm=tm, tn=Qr,
                  epilogue=functools.partial(_ep_rmsnorm, eps=RMS_EPS),
                  extras=(q_norm.reshape(1, Qr),),
                  extra_specs=(pl.BlockSpec((1, Qr), lambda j, i: (0, 0)),), name="mla_dq")
    scale = float(hq) ** -0.5 * LOG2E
    w_q = jnp.pad(w_uq.reshape(Qr, H, hq), ((0, 0), (0, 0), (0, 2 * LANES - hq)))
    w_q = w_q.reshape(Qr, H * 2 * LANES).astype(BF16)
    tabs = _rope_tables_half(S, rope, scale)
    q = _matmul(c_q, w_q, out_dtype=BF16, tm=tm,
                epilogue=functools.partial(_ep_mla_q, scale=scale), extras=(tabs,),
                extra_specs=(pl.BlockSpec((None, 3, tm, LANES), lambda j, i: (0, 0, i % nsb, 0)),),
                name="mla_uq")
    o = _attention(q, kv, kv, B=B, S=S, H=H, n_maps=1, dk=2 * LANES, dv=nope,
                   q_col0=0, k_col0=0, v_col0=H, k_width=nope, kpe=k_pe,
                   finalize=_fin_plain, name="mla_attn")
    return _matmul(o, w_o.astype(BF16), out_dtype=BF16, epilogue=_ep_plain, tm=1024, name="mla_out")


def kernel(x, ln_g, ln_b, a_w_qkv, a_lambda, a_subln, a_w_o, kv_w_a, kv_norm, kv_w_b, b_w_dq, b_q_norm, b_w_uq, b_w_o, moe_w_router, moe_router_bias, moe_w_gate_up, moe_w_down, moe_sh_w_gate_up, moe_sh_w_down):
    B, S, D = x.shape
    depth = ln_g.shape[0]
    n_a = a_w_qkv.shape[0]
    alpha = (2 * depth) ** 0.25
    rope_dim = kv_w_a.shape[1] - kv_norm.shape[0]
    mla_heads = (b_w_uq.shape[2] - kv_w_b.shape[1] + b_w_o.shape[1]) // rope_dim
    h32 = x.reshape(B * S, D)
    hb = h32.astype(BF16)
    kv = k_pe = None
    for l in range(depth):
        if l < n_a:
            lambda_init = 0.8 - 0.6 * math.exp(-0.3 * l)
            y = _diff_attention(hb, a_w_qkv[l], a_lambda[l], a_subln[l], a_w_o[l], lambda_init, B, S)
        else:
            if l == n_a:
                kv, k_pe = _mla_shared_kv(hb, kv_w_a, kv_norm, kv_w_b, mla_heads,
                                          b_w_o.shape[1] // mla_heads, S)
            j = l - n_a
            y = _mla_attention(hb, kv, k_pe, b_w_dq[j], b_q_norm[j], b_w_uq[j], b_w_o[j], mla_heads, B, S)
        h32, hb, hp = _ln_residual(h32, y, ln_g[l, 0], ln_b[l, 0], alpha)
        h32, hb = _moe_ln(h32, hb, hp, l, moe_w_router[l], moe_router_bias[l], moe_w_gate_up, moe_w_down,
                          moe_sh_w_gate_up[l], moe_sh_w_down[l], ln_g[l, 1], ln_b[l, 1], alpha)
    return h32.reshape(B, S, D)
```
